```python
import jax, jax.numpy as jnp
from jax import lax

D_MODEL = 1024
BATCH = 2
SEQ = 16384
DEPTH = 2
DEC_BATCH = 8
DEC_SEQ = 64
PAST_LEN = 1024

CHUNK = 64
BRANCH_W = D_MODEL // 2
N_BRANCH = 3
GMLP_CHUNK = 128
GMLP_GROUPS = 4
GMLP_GW = BRANCH_W // GMLP_GROUPS
CONV_W = 31
N_HEADS = 8
HEAD_DIM = BRANCH_W // N_HEADS
BAND_CHUNKS = 8
WINDOW = BAND_CHUNKS * CHUNK
BAND = WINDOW + CHUNK
MAX_REL = 128
FFN_HIDDEN = ((8 * D_MODEL // 3 + 255) // 256) * 256
IN_COLS = 7 * BRANCH_W + N_BRANCH * D_MODEL
ATTN_SCALE = HEAD_DIM ** -0.5
NEG_INF = -1e30
EPS = 1e-6

kernel_name = "chunk_causal_hybrid_gated_encoder_step"


def rms_norm(x, g):
    xf = x.astype(jnp.float32)
    y = xf * lax.rsqrt(jnp.mean(xf * xf, axis=-1, keepdims=True) + EPS)
    return (y * g.astype(jnp.float32)).astype(x.dtype)


def gmlp_spatial(u, v_n, ws, bs):
    b, l, w = u.shape
    c = min(l, GMLP_CHUNK)
    n = l // c
    vr = v_n.reshape(b, n, c, GMLP_GROUPS, GMLP_GW)
    wm = jnp.tril(ws[:, :c, :c])
    mixed = jnp.einsum('gij,bnjgc->bnigc', wm, vr) + jnp.transpose(bs[:, :c])[None, None, :, :, None]
    return u * mixed.reshape(b, l, w)


def depthwise_causal_conv(x_padded, w, bias):
    y = lax.conv_general_dilated(x_padded, w[:, None, :], window_strides=(1,), padding='VALID',
                                 dimension_numbers=('NWC', 'WIO', 'NWC'), feature_group_count=BRANCH_W)
    return y + bias


def rel_bias_lookup(rel_bias, dist):
    return rel_bias[:, jnp.clip(dist, -MAX_REL, MAX_REL) + MAX_REL]


def band_attention_prompt(q, k, v, rel_bias):
    b, s, h, dh = q.shape
    nc = s // CHUNK
    pad = ((0, 0), (WINDOW, 0), (0, 0), (0, 0))
    kp = jnp.pad(k, pad)
    vp = jnp.pad(v, pad)
    idx = jnp.arange(nc)[:, None] * CHUNK + jnp.arange(BAND)[None, :]
    kb = kp[:, idx]
    vb = vp[:, idx]
    qc = q.reshape(b, nc, CHUNK, h, dh)
    scores = jnp.einsum('bcqhd,bckhd->bhcqk', qc, kb, preferred_element_type=jnp.float32) * ATTN_SCALE
    dist = WINDOW + jnp.arange(CHUNK)[:, None] - jnp.arange(BAND)[None, :]
    scores = scores + rel_bias_lookup(rel_bias, dist).astype(jnp.float32)[None, :, None]
    valid = (idx >= WINDOW)[None, None, :, None, :]
    scores = jnp.where(valid, scores, NEG_INF)
    p = jax.nn.softmax(scores, axis=-1).astype(v.dtype)
    o = jnp.einsum('bhcqk,bckhd->bcqhd', p, vb)
    return o.reshape(b, s, h * dh)


def attention_sample(q, k_all, v_all, rel_bias):
    b, t, h, dh = q.shape
    r = k_all.shape[1] - t
    scores = jnp.einsum('bqhd,bkhd->bhqk', q, k_all, preferred_element_type=jnp.float32) * ATTN_SCALE
    dist = r + jnp.arange(t)[:, None] - jnp.arange(r + t)[None, :]
    scores = scores + rel_bias_lookup(rel_bias, dist).astype(jnp.float32)[None]
    p = jax.nn.softmax(scores, axis=-1).astype(v_all.dtype)
    o = jnp.einsum('bhqk,bkhd->bqhd', p, v_all)
    return o.reshape(b, t, h * dh)


def trunk_layer(x, cache_k, cache_v, cache_conv, norm_mix_g, w_in, b_gate, gmlp_norm_g, gmlp_ws, gmlp_bs,
                conv_dw, conv_b, conv_norm_g, q_norm_g, k_norm_g, rel_bias, w_branch, w_out,
                norm_ffn_g, w_gate_up, w_down):
    b, l, _ = x.shape
    W = BRANCH_W
    h = rms_norm(x, norm_mix_g)
    z = jnp.einsum('bld,dc->blc', h, w_in)
    u = jax.nn.gelu(z[..., 0:W])
    v = jax.nn.gelu(z[..., W:2 * W])
    glu = z[..., 2 * W:3 * W] * jax.nn.sigmoid(z[..., 3 * W:4 * W])
    q = z[..., 4 * W:5 * W].reshape(b, l, N_HEADS, HEAD_DIM)
    k = z[..., 5 * W:6 * W].reshape(b, l, N_HEADS, HEAD_DIM)
    va = z[..., 6 * W:7 * W].reshape(b, l, N_HEADS, HEAD_DIM)
    gates = jax.nn.sigmoid(z[..., 7 * W:] + b_gate).reshape(b, l, N_BRANCH, D_MODEL)

    v_n = rms_norm(v, gmlp_norm_g)
    y_a = gmlp_spatial(u, v_n, gmlp_ws, gmlp_bs)

    if cache_conv is None:
        conv_in = jnp.pad(glu, ((0, 0), (CONV_W - 1, 0), (0, 0)))
    else:
        conv_in = jnp.concatenate([cache_conv, glu], axis=1)
    new_conv = conv_in[:, -(CONV_W - 1):]
    y_b = jax.nn.silu(rms_norm(depthwise_causal_conv(conv_in, conv_dw, conv_b), conv_norm_g))

    q = rms_norm(q, q_norm_g)
    k = rms_norm(k, k_norm_g)
    if cache_k is None:
        y_c = band_attention_prompt(q, k, va, rel_bias)
        keep = min(WINDOW, l)
        new_k = k[:, l - keep:]
        new_v = va[:, l - keep:]
    else:
        k_all = jnp.concatenate([cache_k, k], axis=1)
        v_all = jnp.concatenate([cache_v, va], axis=1)
        y_c = attention_sample(q, k_all, v_all, rel_bias)
        new_k = k
        new_v = va

    merged = (gates[:, :, 0] * jnp.einsum('blw,wd->bld', y_a, w_branch[0])
              + gates[:, :, 1] * jnp.einsum('blw,wd->bld', y_b, w_branch[1])
              + gates[:, :, 2] * jnp.einsum('blw,wd->bld', y_c, w_branch[2]))
    x = x + jnp.einsum('bld,de->ble', merged, w_out)

    h2 = rms_norm(x, norm_ffn_g)
    gu = jnp.einsum('bld,df->blf', h2, w_gate_up)
    x = x + jnp.einsum('blf,fd->bld', jax.nn.silu(gu[..., :FFN_HIDDEN]) * gu[..., FFN_HIDDEN:], w_down)
    return x, new_k, new_v, new_conv, v_n


def setup_inputs(seed: int = 0) -> dict:
    key = jax.random.key(seed)
    ks = jax.random.split(key, 22)
    f32 = jnp.float32
    r = min(WINDOW, PAST_LEN)

    def nrm(k, shape, scale):
        return jax.random.normal(k, shape, f32) * scale

    return {
        "x_prompt": nrm(ks[0], (BATCH, SEQ, D_MODEL), 1.0),
        "x_sample": nrm(ks[1], (DEC_BATCH, DEC_SEQ, D_MODEL), 1.0),
        "cache_attn_k": nrm(ks[2], (DEPTH, DEC_BATCH, r, N_HEADS, HEAD_DIM), 1.0),
        "cache_attn_v": nrm(ks[3], (DEPTH, DEC_BATCH, r, N_HEADS, HEAD_DIM), 1.0),
        "cache_conv": nrm(ks[4], (DEPTH, DEC_BATCH, CONV_W - 1, BRANCH_W), 0.5),
        "norm_mix_g": 1.0 + nrm(ks[5], (DEPTH, D_MODEL), 0.02),
        "w_in": nrm(ks[6], (DEPTH, D_MODEL, IN_COLS), D_MODEL ** -0.5),
        "b_gate": nrm(ks[7], (DEPTH, N_BRANCH * D_MODEL), 0.01),
        "gmlp_norm_g": 1.0 + nrm(ks[8], (DEPTH, BRANCH_W), 0.02),
        "gmlp_ws": nrm(ks[9], (DEPTH, GMLP_GROUPS, GMLP_CHUNK, GMLP_CHUNK), GMLP_CHUNK ** -0.5),
        "gmlp_bs": 1.0 + nrm(ks[10], (DEPTH, GMLP_GROUPS, GMLP_CHUNK), 0.1),
        "conv_dw": nrm(ks[11], (DEPTH, CONV_W, BRANCH_W), CONV_W ** -0.5),
        "conv_b": nrm(ks[12], (DEPTH, BRANCH_W), 0.01),
        "conv_norm_g": 1.0 + nrm(ks[13], (DEPTH, BRANCH_W), 0.02),
        "q_norm_g": 1.0 + nrm(ks[14], (DEPTH, HEAD_DIM), 0.02),
        "k_norm_g": 1.0 + nrm(ks[15], (DEPTH, HEAD_DIM), 0.02),
        "rel_bias": nrm(ks[16], (DEPTH, N_HEADS, 2 * MAX_REL + 1), 0.1),
        "w_branch": nrm(ks[17], (DEPTH, N_BRANCH, BRANCH_W, D_MODEL), BRANCH_W ** -0.5),
        "w_out": nrm(ks[18], (DEPTH, D_MODEL, D_MODEL), D_MODEL ** -0.5),
        "norm_ffn_g": 1.0 + nrm(ks[19], (DEPTH, D_MODEL), 0.02),
        "w_gate_up": nrm(ks[20], (DEPTH, D_MODEL, 2 * FFN_HIDDEN), D_MODEL ** -0.5),
        "w_down": nrm(ks[21], (DEPTH, FFN_HIDDEN, D_MODEL), FFN_HIDDEN ** -0.5),
    }


def reference(x_prompt, x_sample, cache_attn_k, cache_attn_v, cache_conv, norm_mix_g, w_in, b_gate,
              gmlp_norm_g, gmlp_ws, gmlp_bs, conv_dw, conv_b, conv_norm_g, q_norm_g, k_norm_g, rel_bias,
              w_branch, w_out, norm_ffn_g, w_gate_up, w_down):
    xp = x_prompt
    xs = x_sample
    kp_l, vp_l, cp_l, ks_l, vs_l, cs_l, gs_l = [], [], [], [], [], [], []
    for l in range(DEPTH):
        weights = (norm_mix_g[l], w_in[l], b_gate[l], gmlp_norm_g[l], gmlp_ws[l], gmlp_bs[l],
                   conv_dw[l], conv_b[l], conv_norm_g[l], q_norm_g[l], k_norm_g[l], rel_bias[l],
                   w_branch[l], w_out[l], norm_ffn_g[l], w_gate_up[l], w_down[l])
        xp, nk_p, nv_p, nc_p, _ = trunk_layer(xp, None, None, None, *weights)
        xs, nk_s, nv_s, nc_s, gv_s = trunk_layer(xs, cache_attn_k[l], cache_attn_v[l], cache_conv[l], *weights)
        kp_l.append(nk_p); vp_l.append(nv_p); cp_l.append(nc_p)
        ks_l.append(nk_s); vs_l.append(nv_s); cs_l.append(nc_s); gs_l.append(gv_s)
    return (xp, xs, jnp.stack(kp_l), jnp.stack(vp_l), jnp.stack(cp_l),
            jnp.stack(ks_l), jnp.stack(vs_l), jnp.stack(cs_l), jnp.stack(gs_l))
```

```python
import functools

import jax
import jax.numpy as jnp
from jax import lax
from jax.experimental import pallas as pl
from jax.experimental.pallas import tpu as pltpu

D_MODEL = 1024
BRANCH_W = 512
N_BRANCH = 3
GMLP_CHUNK = 128
GMLP_GROUPS = 4
GMLP_GW = BRANCH_W // GMLP_GROUPS
CONV_W = 31
N_HEADS = 8
HEAD_DIM = 64
CHUNK = 64
WINDOW = 512
BAND = WINDOW + CHUNK
MAX_REL = 128
FFN_HIDDEN = 2816
IN_COLS = 7 * BRANCH_W + N_BRANCH * D_MODEL
ATTN_SCALE = HEAD_DIM ** -0.5
NEG_INF = -1e30
EPS = 1e-6

LANES = 128
N_PAIRS = N_HEADS // 2
KPAD = CHUNK
BAND_PAD = BAND + KPAD
CONV_HIST = CONV_W - 1
CONV_OFF = 32
VMEM_LIMIT = 56 * 1024 * 1024

F32 = jnp.float32
BF16 = jnp.bfloat16


def _rms(x, g):
    return x * lax.rsqrt(jnp.mean(x * x, axis=-1, keepdims=True) + EPS) * g


def _mixer_kernel(*refs, T, has_cache, emit_vn, gc, carry):
    it = iter(refs)
    x_ref = next(it)
    if has_cache:
        ck_ref, cv_ref, cc_ref = next(it), next(it), next(it)
    gmix_ref, win_ref, bgate_ref = next(it), next(it), next(it)
    ggmlp_ref, ws_ref, gbias_ref = next(it), next(it), next(it)
    cdw_ref, cb_ref, gconv_ref = next(it), next(it), next(it)
    gq_ref, gk_ref, relb_ref = next(it), next(it), next(it)
    wbr_ref, wout_ref = next(it), next(it)
    xo_ref, kst_ref, vst_ref, cst_ref = next(it), next(it), next(it), next(it)
    if emit_vn:
        vn_ref = next(it)
    kwin, vwin, cbuf, ybuf = next(it), next(it), next(it), next(it)

    t = pl.program_id(1)
    W = BRANCH_W
    hist = KPAD + WINDOW

    @pl.when(t == 0)
    def _init():
        if has_cache:
            kwin[0:KPAD, :] = jnp.zeros((KPAD, W), BF16)
            vwin[0:KPAD, :] = jnp.zeros((KPAD, W), BF16)
            kwin[KPAD:hist, :] = ck_ref[0].astype(BF16)
            vwin[KPAD:hist, :] = cv_ref[0].astype(BF16)
            cbuf[0:CONV_OFF, :] = jnp.zeros((CONV_OFF, W), F32)
            cbuf[CONV_OFF - CONV_HIST:CONV_OFF, :] = cc_ref[0]
        else:
            kwin[0:hist, :] = jnp.zeros((hist, W), BF16)
            vwin[0:hist, :] = jnp.zeros((hist, W), BF16)
            cbuf[0:CONV_OFF, :] = jnp.zeros((CONV_OFF, W), F32)

    x = x_ref[0]
    h = _rms(x, gmix_ref[...]).astype(BF16)

    def sec(lo, width):
        return jnp.dot(h, win_ref[:, lo:lo + width], preferred_element_type=F32)

    def gate(n):
        z = sec(7 * W + n * D_MODEL, D_MODEL) + bgate_ref[:, n * D_MODEL:(n + 1) * D_MODEL]
        return jax.nn.sigmoid(z)

    u = jax.nn.gelu(sec(0, W))
    v_n = _rms(jax.nn.gelu(sec(W, W)), ggmlp_ref[...])
    if emit_vn:
        vn_ref[0] = v_n
    r_i = lax.broadcasted_iota(jnp.int32, (GMLP_GROUPS, gc, gc), 1)
    c_i = lax.broadcasted_iota(jnp.int32, (GMLP_GROUPS, gc, gc), 2)
    wm = jnp.where(r_i >= c_i, ws_ref[...], 0.0).astype(BF16)
    v_nb = v_n.astype(BF16)
    row_blocks = []
    for n in range(T // gc):
        cols = [jnp.dot(wm[g], v_nb[n * gc:(n + 1) * gc, g * GMLP_GW:(g + 1) * GMLP_GW],
                        preferred_element_type=F32) for g in range(GMLP_GROUPS)]
        row_blocks.append(jnp.concatenate(cols, axis=1) + gbias_ref[...])
    mixed = row_blocks[0] if len(row_blocks) == 1 else jnp.concatenate(row_blocks, axis=0)
    y_a = (u * mixed).astype(BF16)
    merged = gate(0) * jnp.dot(y_a, wbr_ref[0], preferred_element_type=F32)

    glu = sec(2 * W, W) * jax.nn.sigmoid(sec(3 * W, W))
    cbuf[CONV_OFF:CONV_OFF + T, :] = glu
    base = CONV_OFF - CONV_HIST
    acc = jnp.broadcast_to(cb_ref[...], (T, W))
    for k in range(CONV_W):
        acc = acc + cbuf[base + k:base + k + T, :] * cdw_ref[k:k + 1, :]
    y_b = jax.nn.silu(_rms(acc, gconv_ref[...])).astype(BF16)
    cst_ref[0] = cbuf[base + T:base + T + CONV_HIST, :]
    cbuf[0:CONV_OFF, :] = cbuf[T:T + CONV_OFF, :]
    merged = merged + gate(1) * jnp.dot(y_b, wbr_ref[1], preferred_element_type=F32)

    lane = lax.broadcasted_iota(jnp.int32, (T, LANES), 1)
    low = lane < HEAD_DIM

    def pair_norm(zp, g):
        sq = zp * zp
        s0 = jnp.sum(jnp.where(low, sq, 0.0), axis=1, keepdims=True)
        s1 = jnp.sum(jnp.where(low, 0.0, sq), axis=1, keepdims=True)
        r = jnp.where(low, lax.rsqrt(s0 * (1.0 / HEAD_DIM) + EPS), lax.rsqrt(s1 * (1.0 / HEAD_DIM) + EPS))
        return zp * r * g

    q = sec(4 * W, W)
    k = sec(5 * W, W)
    va = sec(6 * W, W)
    vst_ref[0] = va
    vwin[hist:hist + T, :] = va.astype(BF16)
    q_pairs = []
    for p in range(N_PAIRS):
        sl = slice(p * LANES, (p + 1) * LANES)
        k_n = pair_norm(k[:, sl], gk_ref[:, sl])
        kst_ref[0, :, sl] = k_n
        kwin[hist:hist + T, sl] = k_n.astype(BF16)
        q_n = pair_norm(q[:, sl], gq_ref[:, sl]) * ATTN_SCALE
        q_pairs.append((jnp.where(low, q_n, 0.0).astype(BF16), jnp.where(low, 0.0, q_n).astype(BF16)))

    col = lax.broadcasted_iota(jnp.int32, (CHUNK, BAND_PAD), 1)
    low_c = lax.broadcasted_iota(jnp.int32, (CHUNK, LANES), 1) < HEAD_DIM
    for c in range(T // CHUNK):
        if not has_cache:
            valid = col >= (hist - t * T - c * CHUNK)
        for p in range(N_PAIRS):
            sl = slice(p * LANES, (p + 1) * LANES)
            kb = kwin[c * CHUNK:c * CHUNK + BAND_PAD, sl]
            vb = vwin[c * CHUNK:c * CHUNK + BAND_PAD, sl]
            outs = []
            for hh in range(2):
                qm = q_pairs[p][hh][c * CHUNK:(c + 1) * CHUNK]
                s = lax.dot_general(qm, kb, (((1,), (1,)), ((), ())), preferred_element_type=F32)
                s = s + relb_ref[2 * p + hh]
                if not has_cache:
                    s = jnp.where(valid, s, NEG_INF)
                e = jnp.exp(s - jnp.max(s, axis=1, keepdims=True))
                den = jnp.sum(e, axis=1, keepdims=True)
                outs.append(jnp.dot(e.astype(BF16), vb, preferred_element_type=F32) / den)
            ybuf[c * CHUNK:(c + 1) * CHUNK, sl] = jnp.where(low_c, outs[0], outs[1]).astype(BF16)
    merged = merged + gate(2) * jnp.dot(ybuf[...], wbr_ref[2], preferred_element_type=F32)

    xo_ref[0] = x + jnp.dot(merged.astype(BF16), wout_ref[...], preferred_element_type=F32)

    if carry:
        kwin[KPAD:hist, :] = kwin[KPAD + T:hist + T, :]
        vwin[KPAD:hist, :] = vwin[KPAD + T:hist + T, :]


def _const_spec(shape):
    zeros = (0,) * len(shape)
    return pl.BlockSpec(shape, lambda b, t: zeros, pipeline_mode=pl.Buffered(1))


def _mixer(x, cache, p, *, T, emit_vn):
    B, L, _ = x.shape
    W = BRANCH_W
    nT = L // T
    keep = min(WINDOW, L)
    nkeep = keep // T
    gc = min(L, GMLP_CHUNK)
    has_cache = cache is not None
    assert L % T == 0 and T % CHUNK == 0 and T % gc == 0 and keep % T == 0 and T <= WINDOW

    tile = lambda b, t: (b, t, 0)
    per_b = lambda b, t: (b, 0, 0)
    state = lambda b, t: (b, jnp.maximum(t - (nT - nkeep), 0), 0)

    args = [x]
    in_specs = [pl.BlockSpec((1, T, D_MODEL), tile)]
    if has_cache:
        ck, cv, cc = cache
        args += [ck.reshape(B, WINDOW, W), cv.reshape(B, WINDOW, W), cc]
        in_specs += [pl.BlockSpec((1, WINDOW, W), per_b), pl.BlockSpec((1, WINDOW, W), per_b),
                     pl.BlockSpec((1, CONV_HIST, W), per_b)]
    consts = [p["g_mix"], p["w_in"], p["b_gate"], p["g_gmlp"], p["ws"][:, :gc, :gc], p["gbias"][:gc],
              p["conv_dw"], p["conv_b"], p["g_conv"], p["g_q"], p["g_k"], p["relb"], p["w_branch"], p["w_out"]]
    args += consts
    in_specs += [_const_spec(a.shape) for a in consts]

    out_shape = [jax.ShapeDtypeStruct((B, L, D_MODEL), F32),
                 jax.ShapeDtypeStruct((B, keep, W), F32),
                 jax.ShapeDtypeStruct((B, keep, W), F32),
                 jax.ShapeDtypeStruct((B, CONV_HIST, W), F32)]
    out_specs = [pl.BlockSpec((1, T, D_MODEL), tile),
                 pl.BlockSpec((1, T, W), state),
                 pl.BlockSpec((1, T, W), state),
                 pl.BlockSpec((1, CONV_HIST, W), per_b)]
    if emit_vn:
        out_shape.append(jax.ShapeDtypeStruct((B, L, W), F32))
        out_specs.append(pl.BlockSpec((1, T, W), tile))

    win_rows = KPAD + WINDOW + T
    return pl.pallas_call(
        functools.partial(_mixer_kernel, T=T, has_cache=has_cache, emit_vn=emit_vn, gc=gc, carry=nT > 1),
        grid=(B, nT),
        in_specs=in_specs,
        out_specs=out_specs,
        out_shape=out_shape,
        scratch_shapes=[pltpu.VMEM((win_rows, W), BF16), pltpu.VMEM((win_rows, W), BF16),
                        pltpu.VMEM((CONV_OFF + T, W), F32), pltpu.VMEM((T, W), BF16)],
        compiler_params=pltpu.CompilerParams(dimension_semantics=("arbitrary", "arbitrary"),
                                             vmem_limit_bytes=VMEM_LIMIT),
        name="mixer_cache" if has_cache else "mixer_prompt",
    )(*args)


def _ffn_kernel(x_ref, g_ref, wgu_ref, wd_ref, o_ref, *, hc):
    x = x_ref[...]
    h2 = _rms(x, g_ref[...]).astype(BF16)
    acc = x
    for j in range(FFN_HIDDEN // hc):
        g = jnp.dot(h2, wgu_ref[:, j * hc:(j + 1) * hc], preferred_element_type=F32)
        u = jnp.dot(h2, wgu_ref[:, FFN_HIDDEN + j * hc:FFN_HIDDEN + (j + 1) * hc], preferred_element_type=F32)
        a = (jax.nn.silu(g) * u).astype(BF16)
        acc = acc + jnp.dot(a, wd_ref[j * hc:(j + 1) * hc, :], preferred_element_type=F32)
    o_ref[...] = acc


def _ffn(x, p, *, T, hc=256):
    B, L, _ = x.shape
    rows = B * L
    assert rows % T == 0 and FFN_HIDDEN % hc == 0
    const = lambda shape: pl.BlockSpec(shape, lambda i: (0,) * len(shape), pipeline_mode=pl.Buffered(1))
    out = pl.pallas_call(
        functools.partial(_ffn_kernel, hc=hc),
        grid=(rows // T,),
        in_specs=[pl.BlockSpec((T, D_MODEL), lambda i: (i, 0)),
                  const((1, D_MODEL)), const((D_MODEL, 2 * FFN_HIDDEN)), const((FFN_HIDDEN, D_MODEL))],
        out_specs=pl.BlockSpec((T, D_MODEL), lambda i: (i, 0)),
        out_shape=jax.ShapeDtypeStruct((rows, D_MODEL), F32),
        compiler_params=pltpu.CompilerParams(dimension_semantics=("arbitrary",),
                                             vmem_limit_bytes=VMEM_LIMIT),
        name="ffn",
    )(x.reshape(rows, D_MODEL), p["g_ffn"], p["w_gate_up"], p["w_down"])
    return out.reshape(B, L, D_MODEL)


def _rel_bias_table(rel_bias):
    j = jnp.arange(BAND_PAD) - KPAD
    dist = WINDOW + jnp.arange(CHUNK)[:, None] - j[None, :]
    tbl = rel_bias[:, jnp.clip(dist, -MAX_REL, MAX_REL) + MAX_REL]
    return jnp.where(j[None, None, :] >= 0, tbl, NEG_INF).astype(F32)


def _layer_params(l, norm_mix_g, w_in, b_gate, gmlp_norm_g, gmlp_ws, gmlp_bs, conv_dw, conv_b, conv_norm_g,
                  q_norm_g, k_norm_g, rel_bias, w_branch, w_out, norm_ffn_g, w_gate_up, w_down):
    row = lambda a: a.reshape(1, -1)
    return {
        "g_mix": row(norm_mix_g[l]),
        "w_in": w_in[l].astype(BF16),
        "b_gate": row(b_gate[l]),
        "g_gmlp": row(gmlp_norm_g[l]),
        "ws": gmlp_ws[l],
        "gbias": jnp.repeat(jnp.transpose(gmlp_bs[l]), GMLP_GW, axis=1),
        "conv_dw": conv_dw[l],
        "conv_b": row(conv_b[l]),
        "g_conv": row(conv_norm_g[l]),
        "g_q": row(jnp.tile(q_norm_g[l], N_HEADS)),
        "g_k": row(jnp.tile(k_norm_g[l], N_HEADS)),
        "relb": _rel_bias_table(rel_bias[l]),
        "w_branch": w_branch[l].astype(BF16),
        "w_out": w_out[l].astype(BF16),
        "g_ffn": row(norm_ffn_g[l]),
        "w_gate_up": w_gate_up[l].astype(BF16),
        "w_down": w_down[l].astype(BF16),
    }


def kernel(x_prompt, x_sample, cache_attn_k, cache_attn_v, cache_conv, norm_mix_g, w_in, b_gate, gmlp_norm_g, gmlp_ws, gmlp_bs, conv_dw, conv_b, conv_norm_g, q_norm_g, k_norm_g, rel_bias, w_branch, w_out, norm_ffn_g, w_gate_up, w_down):
    depth = w_in.shape[0]
    xp, xs = x_prompt, x_sample
    Bp, Bs, Ls = xp.shape[0], xs.shape[0], xs.shape[1]
    kp_l, vp_l, cp_l, ks_l, vs_l, cs_l, gs_l = [], [], [], [], [], [], []
    for l in range(depth):
        p = _layer_params(l, norm_mix_g, w_in, b_gate, gmlp_norm_g, gmlp_ws, gmlp_bs, conv_dw, conv_b,
                          conv_norm_g, q_norm_g, k_norm_g, rel_bias, w_branch, w_out, norm_ffn_g,
                          w_gate_up, w_down)
        xp, kp, vp, cp = _mixer(xp, None, p, T=256, emit_vn=False)
        xp = _ffn(xp, p, T=512)
        xs, ks, vs, cs, gs = _mixer(xs, (cache_attn_k[l], cache_attn_v[l], cache_conv[l]), p, T=Ls, emit_vn=True)
        xs = _ffn(xs, p, T=Bs * Ls)
        kp_l.append(kp.reshape(Bp, -1, N_HEADS, HEAD_DIM))
        vp_l.append(vp.reshape(Bp, -1, N_HEADS, HEAD_DIM))
        cp_l.append(cp)
        ks_l.append(ks.reshape(Bs, Ls, N_HEADS, HEAD_DIM))
        vs_l.append(vs.reshape(Bs, Ls, N_HEADS, HEAD_DIM))
        cs_l.append(cs)
        gs_l.append(gs)
    return (xp, xs, jnp.stack(kp_l), jnp.stack(vp_l), jnp.stack(cp_l),
            jnp.stack(ks_l), jnp.stack(vs_l), jnp.stack(cs_l), jnp.stack(gs_l))
```

```python
import functools
import math

import jax
import jax.numpy as jnp
from jax import lax
from jax.experimental import pallas as pl
from jax.experimental.pallas import tpu as pltpu

D_MODEL = 1024
BRANCH_W = 512
N_BRANCH = 3
GMLP_CHUNK = 128
GMLP_GROUPS = 4
GMLP_GW = BRANCH_W // GMLP_GROUPS
CONV_W = 31
N_HEADS = 8
HEAD_DIM = 64
CHUNK = 64
WINDOW = 512
BAND = WINDOW + CHUNK
MAX_REL = 128
FFN_HIDDEN = 2816
IN_COLS = 7 * BRANCH_W + N_BRANCH * D_MODEL
ATTN_SCALE = HEAD_DIM ** -0.5
NEG_INF = -1e30
EPS = 1e-6

LANES = 128
SUBLANES = 8
N_PAIRS = N_HEADS // 2
KPAD = CHUNK
BAND_PAD = BAND + KPAD
CONV_HIST = CONV_W - 1
CONV_OFF = 32
VMEM_LIMIT = 56 * 1024 * 1024

F32 = jnp.float32
BF16 = jnp.bfloat16
NT_DIMS = (((1,), (1,)), ((), ()))

GELU_A = 2.0 * math.sqrt(2.0 / math.pi)
GELU_B = GELU_A * 0.044715


def _rms(x, g):
    return x * lax.rsqrt(jnp.mean(x * x, axis=-1, keepdims=True) + EPS) * g


def _gelu(x):
    return x * jax.nn.sigmoid(x * (GELU_A + GELU_B * (x * x)))


def _mixer_kernel(*refs, T, has_cache, emit_vn, gc, carry):
    it = iter(refs)
    x_ref = next(it)
    if has_cache:
        ck_ref, cv_ref, cc_ref = next(it), next(it), next(it)
    gmix_ref, win_ref, bgate_ref = next(it), next(it), next(it)
    ggmlp_ref, ws_ref, gbias_ref = next(it), next(it), next(it)
    cdw_ref, cb_ref, gconv_ref = next(it), next(it), next(it)
    gq_ref, gk_ref, relb_ref = next(it), next(it), next(it)
    wbr_ref, wout_ref = next(it), next(it)
    xo_ref, kst_ref, vst_ref, cst_ref = next(it), next(it), next(it), next(it)
    if emit_vn:
        vn_ref = next(it)
    kwin, vwin, cbuf, ybuf = next(it), next(it), next(it), next(it)

    t = pl.program_id(1)
    W = BRANCH_W
    hist = KPAD + WINDOW
    n_chunks = T // CHUNK

    @pl.when(t == 0)
    def _init():
        if has_cache:
            kwin[0:KPAD, :] = jnp.zeros((KPAD, W), BF16)
            vwin[0:KPAD, :] = jnp.zeros((KPAD, W), BF16)
            kwin[KPAD:hist, :] = ck_ref[0].astype(BF16)
            vwin[KPAD:hist, :] = cv_ref[0].astype(BF16)
            cbuf[0:CONV_OFF, :] = jnp.zeros((CONV_OFF, W), F32)
            cbuf[CONV_OFF - CONV_HIST:CONV_OFF, :] = cc_ref[0]
        else:
            kwin[0:hist, :] = jnp.zeros((hist, W), BF16)
            vwin[0:hist, :] = jnp.zeros((hist, W), BF16)
            cbuf[0:CONV_OFF, :] = jnp.zeros((CONV_OFF, W), F32)

    x = x_ref[0]
    h = _rms(x, gmix_ref[...]).astype(BF16)

    def sec(lo, width):
        return jnp.dot(h, win_ref[:, lo:lo + width], preferred_element_type=F32)

    def gate(n):
        z = sec(7 * W + n * D_MODEL, D_MODEL) + bgate_ref[:, n * D_MODEL:(n + 1) * D_MODEL]
        return jax.nn.sigmoid(z)

    glu = sec(2 * W, W) * jax.nn.sigmoid(sec(3 * W, W))
    k = sec(5 * W, W)
    va = sec(6 * W, W)
    q = sec(4 * W, W)
    u = _gelu(sec(0, W))
    v_n = _rms(_gelu(sec(W, W)), ggmlp_ref[...])
    gates = [gate(n) for n in range(N_BRANCH)]

    cbuf[CONV_OFF:CONV_OFF + T, :] = glu
    base = CONV_OFF - CONV_HIST
    parts = []
    for r in range(SUBLANES):
        part = None
        for off in range(base, base + CONV_W):
            if off % SUBLANES != r:
                continue
            term = cbuf[off:off + T, :] * cdw_ref[off - base:off - base + 1, :]
            part = term if part is None else part + term
        if part is not None:
            parts.append(part)
    acc = parts[0]
    for part in parts[1:]:
        acc = acc + part
    acc = acc + cb_ref[...]
    y_b = jax.nn.silu(_rms(acc, gconv_ref[...])).astype(BF16)
    cst_ref[0] = cbuf[base + T:base + T + CONV_HIST, :]
    cbuf[0:CONV_OFF, :] = cbuf[T:T + CONV_OFF, :]

    lane = lax.broadcasted_iota(jnp.int32, (T, LANES), 1)
    low = lane < HEAD_DIM

    def pair_norm(zp, g):
        sq = zp * zp
        s0 = jnp.sum(jnp.where(low, sq, 0.0), axis=1, keepdims=True)
        s1 = jnp.sum(jnp.where(low, 0.0, sq), axis=1, keepdims=True)
        r = jnp.where(low, lax.rsqrt(s0 * (1.0 / HEAD_DIM) + EPS), lax.rsqrt(s1 * (1.0 / HEAD_DIM) + EPS))
        return zp * r * g

    vst_ref[0] = va
    vwin[hist:hist + T, :] = va.astype(BF16)
    q_lo, q_hi = [], []
    for p in range(N_PAIRS):
        sl = slice(p * LANES, (p + 1) * LANES)
        k_n = pair_norm(k[:, sl], gk_ref[:, sl])
        kst_ref[0, :, sl] = k_n
        kwin[hist:hist + T, sl] = k_n.astype(BF16)
        q_n = pair_norm(q[:, sl], gq_ref[:, sl]) * ATTN_SCALE
        q_lo.append(jnp.where(low, q_n, 0.0).astype(BF16))
        q_hi.append(jnp.where(low, 0.0, q_n).astype(BF16))

    def scores(c):
        rows = slice(c * CHUNK, (c + 1) * CHUNK)
        out = []
        for p in range(N_PAIRS):
            q2 = jnp.concatenate([q_lo[p][rows], q_hi[p][rows]], axis=0)
            kb = kwin[c * CHUNK:c * CHUNK + BAND_PAD, p * LANES:(p + 1) * LANES]
            out.append(lax.dot_general(q2, kb, NT_DIMS, preferred_element_type=F32))
        return out

    col = lax.broadcasted_iota(jnp.int32, (2 * CHUNK, BAND_PAD), 1)
    low_c = lax.broadcasted_iota(jnp.int32, (CHUNK, LANES), 1) < HEAD_DIM

    def attend(c, s_pairs):
        if not has_cache:
            valid = col >= (hist - t * T - c * CHUNK)
        for p in range(N_PAIRS):
            sl = slice(p * LANES, (p + 1) * LANES)
            s = s_pairs[p] + relb_ref[p]
            if not has_cache:
                s = jnp.where(valid, s, NEG_INF)
            e = jnp.exp(s - jnp.max(s, axis=1, keepdims=True))
            den = jnp.sum(e, axis=1, keepdims=True)
            vb = vwin[c * CHUNK:c * CHUNK + BAND_PAD, sl]
            o2 = jnp.dot(e.astype(BF16), vb, preferred_element_type=F32) / den
            ybuf[c * CHUNK:(c + 1) * CHUNK, sl] = jnp.where(low_c, o2[0:CHUNK], o2[CHUNK:2 * CHUNK]).astype(BF16)

    s_next = scores(0)
    for c in range(n_chunks):
        s_cur = s_next
        if c + 1 < n_chunks:
            s_next = scores(c + 1)
        attend(c, s_cur)

    if emit_vn:
        vn_ref[0] = v_n
    r_i = lax.broadcasted_iota(jnp.int32, (GMLP_GROUPS, gc, gc), 1)
    c_i = lax.broadcasted_iota(jnp.int32, (GMLP_GROUPS, gc, gc), 2)
    wm = jnp.where(r_i >= c_i, ws_ref[...], 0.0).astype(BF16)
    v_nb = v_n.astype(BF16)
    row_blocks = []
    for n in range(T // gc):
        cols = [jnp.dot(wm[g], v_nb[n * gc:(n + 1) * gc, g * GMLP_GW:(g + 1) * GMLP_GW],
                        preferred_element_type=F32) for g in range(GMLP_GROUPS)]
        row_blocks.append(jnp.concatenate(cols, axis=1) + gbias_ref[...])
    mixed = row_blocks[0] if len(row_blocks) == 1 else jnp.concatenate(row_blocks, axis=0)
    y_a = (u * mixed).astype(BF16)

    merged = (gates[0] * jnp.dot(y_a, wbr_ref[0], preferred_element_type=F32)
              + gates[1] * jnp.dot(y_b, wbr_ref[1], preferred_element_type=F32)
              + gates[2] * jnp.dot(ybuf[...], wbr_ref[2], preferred_element_type=F32))
    xo_ref[0] = x + jnp.dot(merged.astype(BF16), wout_ref[...], preferred_element_type=F32)

    if carry:
        kwin[KPAD:hist, :] = kwin[KPAD + T:hist + T, :]
        vwin[KPAD:hist, :] = vwin[KPAD + T:hist + T, :]


def _const_spec(shape):
    zeros = (0,) * len(shape)
    return pl.BlockSpec(shape, lambda b, t: zeros, pipeline_mode=pl.Buffered(1))


def _mixer(x, cache, p, *, T, emit_vn):
    B, L, _ = x.shape
    W = BRANCH_W
    nT = L // T
    keep = min(WINDOW, L)
    nkeep = keep // T
    gc = min(L, GMLP_CHUNK)
    has_cache = cache is not None
    assert L % T == 0 and T % CHUNK == 0 and T % gc == 0 and keep % T == 0 and T <= WINDOW

    tile = lambda b, t: (b, t, 0)
    per_b = lambda b, t: (b, 0, 0)
    state = lambda b, t: (b, jnp.maximum(t - (nT - nkeep), 0), 0)

    args = [x]
    in_specs = [pl.BlockSpec((1, T, D_MODEL), tile)]
    if has_cache:
        ck, cv, cc = cache
        args += [ck.reshape(B, WINDOW, W), cv.reshape(B, WINDOW, W), cc]
        in_specs += [pl.BlockSpec((1, WINDOW, W), per_b), pl.BlockSpec((1, WINDOW, W), per_b),
                     pl.BlockSpec((1, CONV_HIST, W), per_b)]
    consts = [p["g_mix"], p["w_in"], p["b_gate"], p["g_gmlp"], p["ws"][:, :gc, :gc], p["gbias"][:gc],
              p["conv_dw"], p["conv_b"], p["g_conv"], p["g_q"], p["g_k"], p["relb"], p["w_branch"], p["w_out"]]
    args += consts
    in_specs += [_const_spec(a.shape) for a in consts]

    out_shape = [jax.ShapeDtypeStruct((B, L, D_MODEL), F32),
                 jax.ShapeDtypeStruct((B, keep, W), F32),
                 jax.ShapeDtypeStruct((B, keep, W), F32),
                 jax.ShapeDtypeStruct((B, CONV_HIST, W), F32)]
    out_specs = [pl.BlockSpec((1, T, D_MODEL), tile),
                 pl.BlockSpec((1, T, W), state),
                 pl.BlockSpec((1, T, W), state),
                 pl.BlockSpec((1, CONV_HIST, W), per_b)]
    if emit_vn:
        out_shape.append(jax.ShapeDtypeStruct((B, L, W), F32))
        out_specs.append(pl.BlockSpec((1, T, W), tile))

    win_rows = KPAD + WINDOW + T
    return pl.pallas_call(
        functools.partial(_mixer_kernel, T=T, has_cache=has_cache, emit_vn=emit_vn, gc=gc, carry=nT > 1),
        grid=(B, nT),
        in_specs=in_specs,
        out_specs=out_specs,
        out_shape=out_shape,
        scratch_shapes=[pltpu.VMEM((win_rows, W), BF16), pltpu.VMEM((win_rows, W), BF16),
                        pltpu.VMEM((CONV_OFF + T, W), F32), pltpu.VMEM((T, W), BF16)],
        compiler_params=pltpu.CompilerParams(dimension_semantics=("arbitrary", "arbitrary"),
                                             vmem_limit_bytes=VMEM_LIMIT),
        name="mixer_cache" if has_cache else "mixer_prompt",
    )(*args)


def _ffn_kernel(x_ref, g_ref, wgu_ref, wd_ref, o_ref, *, hc):
    x = x_ref[...]
    h2 = _rms(x, g_ref[...]).astype(BF16)
    acc = x
    for j in range(FFN_HIDDEN // hc):
        g = jnp.dot(h2, wgu_ref[:, j * hc:(j + 1) * hc], preferred_element_type=F32)
        u = jnp.dot(h2, wgu_ref[:, FFN_HIDDEN + j * hc:FFN_HIDDEN + (j + 1) * hc], preferred_element_type=F32)
        a = (jax.nn.silu(g) * u).astype(BF16)
        acc = acc + jnp.dot(a, wd_ref[j * hc:(j + 1) * hc, :], preferred_element_type=F32)
    o_ref[...] = acc


def _ffn(x, p, *, T, hc=256):
    B, L, _ = x.shape
    rows = B * L
    assert rows % T == 0 and FFN_HIDDEN % hc == 0
    const = lambda shape: pl.BlockSpec(shape, lambda i: (0,) * len(shape), pipeline_mode=pl.Buffered(1))
    out = pl.pallas_call(
        functools.partial(_ffn_kernel, hc=hc),
        grid=(rows // T,),
        in_specs=[pl.BlockSpec((T, D_MODEL), lambda i: (i, 0)),
                  const((1, D_MODEL)), const((D_MODEL, 2 * FFN_HIDDEN)), const((FFN_HIDDEN, D_MODEL))],
        out_specs=pl.BlockSpec((T, D_MODEL), lambda i: (i, 0)),
        out_shape=jax.ShapeDtypeStruct((rows, D_MODEL), F32),
        compiler_params=pltpu.CompilerParams(dimension_semantics=("arbitrary",),
                                             vmem_limit_bytes=VMEM_LIMIT),
        name="ffn",
    )(x.reshape(rows, D_MODEL), p["g_ffn"], p["w_gate_up"], p["w_down"])
    return out.reshape(B, L, D_MODEL)


def _rel_bias_table(rel_bias):
    n_far = BAND_PAD - MAX_REL
    period = BAND_PAD + CHUNK
    far = jnp.broadcast_to(rel_bias[:, 2 * MAX_REL:2 * MAX_REL + 1], (N_HEADS, n_far))
    near = jnp.flip(rel_bias[:, 2 * MAX_REL - (period - n_far):2 * MAX_REL], axis=1)
    g = jnp.concatenate([far, near], axis=1)
    skew = jnp.tile(g, (1, CHUNK))[:, :CHUNK * (period - 1)].reshape(N_HEADS, CHUNK, period - 1)
    tbl = skew[:, :, CHUNK - 1:CHUNK - 1 + BAND_PAD]
    tbl = jnp.where(jnp.arange(BAND_PAD)[None, None, :] >= KPAD, tbl, NEG_INF)
    return tbl.reshape(N_PAIRS, 2 * CHUNK, BAND_PAD).astype(F32)


def _layer_params(l, norm_mix_g, w_in, b_gate, gmlp_norm_g, gmlp_ws, gmlp_bs, conv_dw, conv_b, conv_norm_g,
                  q_norm_g, k_norm_g, rel_bias, w_branch, w_out, norm_ffn_g, w_gate_up, w_down):
    row = lambda a: a.reshape(1, -1)
    return {
        "g_mix": row(norm_mix_g[l]),
        "w_in": w_in[l].astype(BF16),
        "b_gate": row(b_gate[l]),
        "g_gmlp": row(gmlp_norm_g[l]),
        "ws": gmlp_ws[l],
        "gbias": jnp.repeat(jnp.transpose(gmlp_bs[l]), GMLP_GW, axis=1),
        "conv_dw": conv_dw[l],
        "conv_b": row(conv_b[l]),
        "g_conv": row(conv_norm_g[l]),
        "g_q": row(jnp.tile(q_norm_g[l], N_HEADS)),
        "g_k": row(jnp.tile(k_norm_g[l], N_HEADS)),
        "relb": _rel_bias_table(rel_bias[l]),
        "w_branch": w_branch[l].astype(BF16),
        "w_out": w_out[l].astype(BF16),
        "g_ffn": row(norm_ffn_g[l]),
        "w_gate_up": w_gate_up[l].astype(BF16),
        "w_down": w_down[l].astype(BF16),
    }


def kernel(x_prompt, x_sample, cache_attn_k, cache_attn_v, cache_conv, norm_mix_g, w_in, b_gate, gmlp_norm_g, gmlp_ws, gmlp_bs, conv_dw, conv_b, conv_norm_g, q_norm_g, k_norm_g, rel_bias, w_branch, w_out, norm_ffn_g, w_gate_up, w_down):
    depth = w_in.shape[0]
    xp, xs = x_prompt, x_sample
    Bp, Bs, Ls = xp.shape[0], xs.shape[0], xs.shape[1]
    kp_l, vp_l, cp_l, ks_l, vs_l, cs_l, gs_l = [], [], [], [], [], [], []
    for l in range(depth):
        p = _layer_params(l, norm_mix_g, w_in, b_gate, gmlp_norm_g, gmlp_ws, gmlp_bs, conv_dw, conv_b,
                          conv_norm_g, q_norm_g, k_norm_g, rel_bias, w_branch, w_out, norm_ffn_g,
                          w_gate_up, w_down)
        xp, kp, vp, cp = _mixer(xp, None, p, T=256, emit_vn=False)
        xp = _ffn(xp, p, T=512)
        xs, ks, vs, cs, gs = _mixer(xs, (cache_attn_k[l], cache_attn_v[l], cache_conv[l]), p, T=Ls, emit_vn=True)
        xs = _ffn(xs, p, T=Bs * Ls)
        kp_l.append(kp.reshape(Bp, -1, N_HEADS, HEAD_DIM))
        vp_l.append(vp.reshape(Bp, -1, N_HEADS, HEAD_DIM))
        cp_l.append(cp)
        ks_l.append(ks.reshape(Bs, Ls, N_HEADS, HEAD_DIM))
        vs_l.append(vs.reshape(Bs, Ls, N_HEADS, HEAD_DIM))
        cs_l.append(cs)
        gs_l.append(gs)
    return (xp, xs, jnp.stack(kp_l), jnp.stack(vp_l), jnp.stack(cp_l),
            jnp.stack(ks_l), jnp.stack(vs_l), jnp.stack(cs_l), jnp.stack(gs_l))
```

```python
import functools
import math

import jax
import jax.numpy as jnp
from jax import lax
from jax.experimental import pallas as pl
from jax.experimental.pallas import tpu as pltpu

D_MODEL = 1024
BRANCH_W = 512
N_BRANCH = 3
GMLP_CHUNK = 128
GMLP_GROUPS = 4
GMLP_GW = BRANCH_W // GMLP_GROUPS
CONV_W = 31
N_HEADS = 8
HEAD_DIM = 64
CHUNK = 64
WINDOW = 512
BAND = WINDOW + CHUNK
MAX_REL = 128
FFN_HIDDEN = 2816
IN_COLS = 7 * BRANCH_W + N_BRANCH * D_MODEL
ATTN_SCALE = HEAD_DIM ** -0.5
NEG_INF = -1e30
EPS = 1e-6

LANES = 128
SUBLANES = 8
N_PAIRS = N_HEADS // 2
KPAD = CHUNK
BAND_PAD = BAND + KPAD
CONV_HIST = CONV_W - 1
CONV_OFF = 32
VMEM_LIMIT = 56 * 1024 * 1024
MIXER_TILE = 512
FFN_TILE = 512
FFN_HC = 256

F32 = jnp.float32
BF16 = jnp.bfloat16
NT_DIMS = (((1,), (1,)), ((), ()))

GELU_A = 2.0 * math.sqrt(2.0 / math.pi)
GELU_B = GELU_A * 0.044715


def _rms(x, g):
    return x * lax.rsqrt(jnp.mean(x * x, axis=-1, keepdims=True) + EPS) * g


def _gelu(x):
    return x * jax.nn.sigmoid(x * (GELU_A + GELU_B * (x * x)))


def _mixer_kernel(*refs, T, has_cache, emit_vn, gc, carry):
    it = iter(refs)
    x_ref = next(it)
    if has_cache:
        ck_ref, cv_ref, cc_ref = next(it), next(it), next(it)
    gmix_ref, win_ref, bgate_ref = next(it), next(it), next(it)
    ggmlp_ref, ws_ref, gbias_ref = next(it), next(it), next(it)
    cdw_ref, cb_ref, gconv_ref = next(it), next(it), next(it)
    gq_ref, gk_ref, relb_ref = next(it), next(it), next(it)
    wbr_ref, wout_ref = next(it), next(it)
    xo_ref, kst_ref, vst_ref, cst_ref = next(it), next(it), next(it), next(it)
    if emit_vn:
        vn_ref = next(it)
    kwin, vwin, cbuf, ybuf = next(it), next(it), next(it), next(it)

    t = pl.program_id(1)
    W = BRANCH_W
    hist = KPAD + WINDOW
    n_chunks = T // CHUNK
    base = CONV_OFF - CONV_HIST

    @pl.when(t == 0)
    def _init():
        if has_cache:
            kwin[0:KPAD, :] = jnp.zeros((KPAD, W), BF16)
            vwin[0:KPAD, :] = jnp.zeros((KPAD, W), BF16)
            kwin[KPAD:hist, :] = ck_ref[0].astype(BF16)
            vwin[KPAD:hist, :] = cv_ref[0].astype(BF16)
            cbuf[0:CONV_OFF, :] = jnp.zeros((CONV_OFF, W), F32)
            cbuf[base:CONV_OFF, :] = cc_ref[0]
        else:
            kwin[0:hist, :] = jnp.zeros((hist, W), BF16)
            vwin[0:hist, :] = jnp.zeros((hist, W), BF16)
            cbuf[0:CONV_OFF, :] = jnp.zeros((CONV_OFF, W), F32)

    h = _rms(x_ref[0], gmix_ref[...]).astype(BF16)

    def sec(lo, width):
        return jnp.dot(h, win_ref[:, lo:lo + width], preferred_element_type=F32)

    def gate_pre(n):
        return sec(7 * W + n * D_MODEL, D_MODEL) + bgate_ref[:, n * D_MODEL:(n + 1) * D_MODEL]

    def conv_part(r):
        part = None
        for off in range(base, base + CONV_W):
            if off % SUBLANES == r:
                term = cbuf[off:off + T, :] * cdw_ref[off - base:off - base + 1, :]
                part = term if part is None else part + term
        return part

    low = lax.broadcasted_iota(jnp.int32, (T, LANES), 1) < HEAD_DIM

    def pair_norm(zp, g):
        sq = zp * zp
        s0 = jnp.sum(jnp.where(low, sq, 0.0), axis=1, keepdims=True)
        s1 = jnp.sum(jnp.where(low, 0.0, sq), axis=1, keepdims=True)
        r = jnp.where(low, lax.rsqrt(s0 * (1.0 / HEAD_DIM) + EPS), lax.rsqrt(s1 * (1.0 / HEAD_DIM) + EPS))
        return zp * r * g

    def scores(c):
        rows = slice(c * CHUNK, (c + 1) * CHUNK)
        out = []
        for p in range(N_PAIRS):
            q2 = jnp.concatenate([q_lo[p][rows], q_hi[p][rows]], axis=0)
            kb = kwin[c * CHUNK:c * CHUNK + BAND_PAD, p * LANES:(p + 1) * LANES]
            out.append(lax.dot_general(q2, kb, NT_DIMS, preferred_element_type=F32))
        return out

    col = lax.broadcasted_iota(jnp.int32, (2 * CHUNK, BAND_PAD), 1)
    low_c = lax.broadcasted_iota(jnp.int32, (CHUNK, LANES), 1) < HEAD_DIM

    def softmax(c, s_pairs):
        if not has_cache:
            valid = col >= (hist - t * T - c * CHUNK)
        out = []
        for p in range(N_PAIRS):
            s = s_pairs[p] + relb_ref[p]
            if not has_cache:
                s = jnp.where(valid, s, NEG_INF)
            e = jnp.exp(s - jnp.max(s, axis=1, keepdims=True))
            out.append((e.astype(BF16), jnp.sum(e, axis=1, keepdims=True)))
        return out

    def pv(c, e_pairs):
        for p in range(N_PAIRS):
            sl = slice(p * LANES, (p + 1) * LANES)
            e, den = e_pairs[p]
            vb = vwin[c * CHUNK:c * CHUNK + BAND_PAD, sl]
            o2 = jnp.dot(e, vb, preferred_element_type=F32) / den
            ybuf[c * CHUNK:(c + 1) * CHUNK, sl] = jnp.where(low_c, o2[0:CHUNK], o2[CHUNK:2 * CHUNK]).astype(BF16)

    z_a = sec(2 * W, W)
    z_b = sec(3 * W, W)
    cbuf[CONV_OFF:CONV_OFF + T, :] = z_a * jax.nn.sigmoid(z_b)
    k = sec(5 * W, W)
    acc = conv_part(0) + conv_part(1)
    va = sec(6 * W, W)
    acc = acc + conv_part(2) + conv_part(3)
    for p in range(N_PAIRS):
        sl = slice(p * LANES, (p + 1) * LANES)
        k_n = pair_norm(k[:, sl], gk_ref[:, sl])
        kst_ref[0, :, sl] = k_n
        kwin[hist:hist + T, sl] = k_n.astype(BF16)
    q = sec(4 * W, W)
    acc = acc + conv_part(4) + conv_part(5)
    vst_ref[0] = va
    vwin[hist:hist + T, :] = va.astype(BF16)
    z_u = sec(0, W)
    acc = acc + conv_part(6) + conv_part(7)
    q_lo, q_hi = [], []
    for p in range(N_PAIRS):
        sl = slice(p * LANES, (p + 1) * LANES)
        q_n = pair_norm(q[:, sl], gq_ref[:, sl]) * ATTN_SCALE
        q_lo.append(jnp.where(low, q_n, 0.0).astype(BF16))
        q_hi.append(jnp.where(low, 0.0, q_n).astype(BF16))
    z_v = sec(W, W)
    y_b = jax.nn.silu(_rms(acc + cb_ref[...], gconv_ref[...])).astype(BF16)
    cst_ref[0] = cbuf[base + T:base + T + CONV_HIST, :]
    cbuf[0:CONV_OFF, :] = cbuf[T:T + CONV_OFF, :]
    u = _gelu(z_u)
    g_pre0 = gate_pre(0)
    v_n = _rms(_gelu(z_v), ggmlp_ref[...])
    if emit_vn:
        vn_ref[0] = v_n
    v_nb = v_n.astype(BF16)
    g_pre1 = gate_pre(1)
    gate0 = jax.nn.sigmoid(g_pre0)
    g_pre2 = gate_pre(2)
    gate1 = jax.nn.sigmoid(g_pre1)

    s_pairs = scores(0)
    gate2 = jax.nn.sigmoid(g_pre2)
    r_i = lax.broadcasted_iota(jnp.int32, (GMLP_GROUPS, gc, gc), 1)
    c_i = lax.broadcasted_iota(jnp.int32, (GMLP_GROUPS, gc, gc), 2)
    wm = jnp.where(r_i >= c_i, ws_ref[...], 0.0).astype(BF16)
    row_blocks = []
    for n in range(T // gc):
        cols = [jnp.dot(wm[g], v_nb[n * gc:(n + 1) * gc, g * GMLP_GW:(g + 1) * GMLP_GW],
                        preferred_element_type=F32) for g in range(GMLP_GROUPS)]
        row_blocks.append(jnp.concatenate(cols, axis=1) + gbias_ref[...])
    mixed = row_blocks[0] if len(row_blocks) == 1 else jnp.concatenate(row_blocks, axis=0)
    y_a = (u * mixed).astype(BF16)
    merged = None
    for c in range(n_chunks):
        s_next = scores(c + 1) if c + 1 < n_chunks else None
        e_pairs = softmax(c, s_pairs)
        if c == 0:
            merged = gate0 * jnp.dot(y_a, wbr_ref[0], preferred_element_type=F32)
        if c == min(1, n_chunks - 1):
            merged = merged + gate1 * jnp.dot(y_b, wbr_ref[1], preferred_element_type=F32)
        pv(c, e_pairs)
        s_pairs = s_next

    merged = merged + gate2 * jnp.dot(ybuf[...], wbr_ref[2], preferred_element_type=F32)
    xo_ref[0] = x_ref[0] + jnp.dot(merged.astype(BF16), wout_ref[...], preferred_element_type=F32)

    if carry:
        kwin[KPAD:hist, :] = kwin[KPAD + T:hist + T, :]
        vwin[KPAD:hist, :] = vwin[KPAD + T:hist + T, :]


def _mixer(x, cache, p, layer, *, T, emit_vn):
    B, L, _ = x.shape
    W = BRANCH_W
    nT = L // T
    keep = min(WINDOW, L)
    nkeep = keep // T
    gc = min(L, GMLP_CHUNK)
    has_cache = cache is not None
    assert L % T == 0 and T % CHUNK == 0 and T % gc == 0 and keep % T == 0 and T <= WINDOW

    tile = lambda b, t: (b, t, 0)
    per_b = lambda b, t: (b, 0, 0)
    state = lambda b, t: (b, jnp.maximum(t - (nT - nkeep), 0), 0)

    def const(a):
        zeros = (0,) * a.ndim
        return pl.BlockSpec(a.shape, lambda b, t: zeros, pipeline_mode=pl.Buffered(1))

    def layer_of(a):
        zeros = (0,) * (a.ndim - 1)
        return pl.BlockSpec((None,) + a.shape[1:], lambda b, t: (layer,) + zeros, pipeline_mode=pl.Buffered(1))

    args = [x]
    in_specs = [pl.BlockSpec((1, T, D_MODEL), tile)]
    if has_cache:
        ck, cv, cc = cache
        args += [ck.reshape(B, WINDOW, W), cv.reshape(B, WINDOW, W), cc]
        in_specs += [pl.BlockSpec((1, WINDOW, W), per_b), pl.BlockSpec((1, WINDOW, W), per_b),
                     pl.BlockSpec((1, CONV_HIST, W), per_b)]
    small = [p["g_mix"], None, p["b_gate"], p["g_gmlp"], p["ws"][:, :gc, :gc], p["gbias"][:gc],
             p["conv_dw"], p["conv_b"], p["g_conv"], p["g_q"], p["g_k"], p["relb"], None, None]
    stacked = {1: p["w_in"], 12: p["w_branch"], 13: p["w_out"]}
    for i, a in enumerate(small):
        if a is None:
            args.append(stacked[i])
            in_specs.append(layer_of(stacked[i]))
        else:
            args.append(a)
            in_specs.append(const(a))

    out_shape = [jax.ShapeDtypeStruct((B, L, D_MODEL), F32),
                 jax.ShapeDtypeStruct((B, keep, W), F32),
                 jax.ShapeDtypeStruct((B, keep, W), F32),
                 jax.ShapeDtypeStruct((B, CONV_HIST, W), F32)]
    out_specs = [pl.BlockSpec((1, T, D_MODEL), tile),
                 pl.BlockSpec((1, T, W), state),
                 pl.BlockSpec((1, T, W), state),
                 pl.BlockSpec((1, CONV_HIST, W), per_b)]
    if emit_vn:
        out_shape.append(jax.ShapeDtypeStruct((B, L, W), F32))
        out_specs.append(pl.BlockSpec((1, T, W), tile))

    win_rows = KPAD + WINDOW + T
    return pl.pallas_call(
        functools.partial(_mixer_kernel, T=T, has_cache=has_cache, emit_vn=emit_vn, gc=gc, carry=nT > 1),
        grid=(B, nT),
        in_specs=in_specs,
        out_specs=out_specs,
        out_shape=out_shape,
        scratch_shapes=[pltpu.VMEM((win_rows, W), BF16), pltpu.VMEM((win_rows, W), BF16),
                        pltpu.VMEM((CONV_OFF + T, W), F32), pltpu.VMEM((T, W), BF16)],
        compiler_params=pltpu.CompilerParams(dimension_semantics=("arbitrary", "arbitrary"),
                                             vmem_limit_bytes=VMEM_LIMIT),
        name="mixer_cache" if has_cache else "mixer_prompt",
    )(*args)


def _ffn_kernel(x_ref, g_ref, wgu_ref, wd_ref, o_ref):
    x = x_ref[...]
    h2 = _rms(x, g_ref[...]).astype(BF16)
    acc = x
    for j in range(FFN_HIDDEN // FFN_HC):
        lo = j * FFN_HC
        g = jnp.dot(h2, wgu_ref[:, lo:lo + FFN_HC], preferred_element_type=F32)
        u = jnp.dot(h2, wgu_ref[:, FFN_HIDDEN + lo:FFN_HIDDEN + lo + FFN_HC], preferred_element_type=F32)
        a = (jax.nn.silu(g) * u).astype(BF16)
        acc = acc + jnp.dot(a, wd_ref[lo:lo + FFN_HC, :], preferred_element_type=F32)
    o_ref[...] = acc


def _ffn(x, p, layer, *, T):
    B, L, _ = x.shape
    rows = B * L
    assert rows % T == 0 and FFN_HIDDEN % FFN_HC == 0
    layer_of = lambda a: pl.BlockSpec((None,) + a.shape[1:], lambda i: (layer, 0, 0), pipeline_mode=pl.Buffered(1))
    out = pl.pallas_call(
        _ffn_kernel,
        grid=(rows // T,),
        in_specs=[pl.BlockSpec((T, D_MODEL), lambda i: (i, 0)),
                  pl.BlockSpec((1, D_MODEL), lambda i: (0, 0), pipeline_mode=pl.Buffered(1)),
                  layer_of(p["w_gate_up"]), layer_of(p["w_down"])],
        out_specs=pl.BlockSpec((T, D_MODEL), lambda i: (i, 0)),
        out_shape=jax.ShapeDtypeStruct((rows, D_MODEL), F32),
        compiler_params=pltpu.CompilerParams(dimension_semantics=("arbitrary",),
                                             vmem_limit_bytes=VMEM_LIMIT),
        name="ffn",
    )(x.reshape(rows, D_MODEL), p["g_ffn"], p["w_gate_up"], p["w_down"])
    return out.reshape(B, L, D_MODEL)


def _rel_bias_table(rel_bias):
    n_far = BAND_PAD - MAX_REL
    period = BAND_PAD + CHUNK
    far = jnp.broadcast_to(rel_bias[:, 2 * MAX_REL:2 * MAX_REL + 1], (N_HEADS, n_far))
    near = jnp.flip(rel_bias[:, 2 * MAX_REL - (period - n_far):2 * MAX_REL], axis=1)
    g = jnp.concatenate([far, near], axis=1)
    skew = jnp.tile(g, (1, CHUNK))[:, :CHUNK * (period - 1)].reshape(N_HEADS, CHUNK, period - 1)
    tbl = skew[:, :, CHUNK - 1:CHUNK - 1 + BAND_PAD]
    tbl = jnp.where(jnp.arange(BAND_PAD)[None, None, :] >= KPAD, tbl, NEG_INF)
    return tbl.reshape(N_PAIRS, 2 * CHUNK, BAND_PAD).astype(F32)


def kernel(x_prompt, x_sample, cache_attn_k, cache_attn_v, cache_conv, norm_mix_g, w_in, b_gate, gmlp_norm_g, gmlp_ws, gmlp_bs, conv_dw, conv_b, conv_norm_g, q_norm_g, k_norm_g, rel_bias, w_branch, w_out, norm_ffn_g, w_gate_up, w_down):
    depth = w_in.shape[0]
    xp, xs = x_prompt, x_sample
    Bp, Bs, Ls = xp.shape[0], xs.shape[0], xs.shape[1]
    row = lambda a: a.reshape(1, -1)
    stacked = {"w_in": w_in.astype(BF16), "w_branch": w_branch.astype(BF16), "w_out": w_out.astype(BF16),
               "w_gate_up": w_gate_up.astype(BF16), "w_down": w_down.astype(BF16)}
    kp_l, vp_l, cp_l, ks_l, vs_l, cs_l, gs_l = [], [], [], [], [], [], []
    for l in range(depth):
        p = dict(stacked)
        p.update({
            "g_mix": row(norm_mix_g[l]),
            "b_gate": row(b_gate[l]),
            "g_gmlp": row(gmlp_norm_g[l]),
            "ws": gmlp_ws[l],
            "gbias": jnp.repeat(jnp.transpose(gmlp_bs[l]), GMLP_GW, axis=1),
            "conv_dw": conv_dw[l],
            "conv_b": row(conv_b[l]),
            "g_conv": row(conv_norm_g[l]),
            "g_q": row(jnp.tile(q_norm_g[l], N_HEADS)),
            "g_k": row(jnp.tile(k_norm_g[l], N_HEADS)),
            "relb": _rel_bias_table(rel_bias[l]),
            "g_ffn": row(norm_ffn_g[l]),
        })
        xp, kp, vp, cp = _mixer(xp, None, p, l, T=MIXER_TILE, emit_vn=False)
        xp = _ffn(xp, p, l, T=FFN_TILE)
        xs, ks, vs, cs, gs = _mixer(xs, (cache_attn_k[l], cache_attn_v[l], cache_conv[l]), p, l, T=Ls, emit_vn=True)
        xs = _ffn(xs, p, l, T=Bs * Ls)
        kp_l.append(kp.reshape(Bp, -1, N_HEADS, HEAD_DIM))
        vp_l.append(vp.reshape(Bp, -1, N_HEADS, HEAD_DIM))
        cp_l.append(cp)
        ks_l.append(ks.reshape(Bs, Ls, N_HEADS, HEAD_DIM))
        vs_l.append(vs.reshape(Bs, Ls, N_HEADS, HEAD_DIM))
        cs_l.append(cs)
        gs_l.append(gs)
    return (xp, xs, jnp.stack(kp_l), jnp.stack(vp_l), jnp.stack(cp_l),
            jnp.stack(ks_l), jnp.stack(vs_l), jnp.stack(cs_l), jnp.stack(gs_l))
```

```python
import functools
import math

import jax
import jax.numpy as jnp
from jax import lax
from jax.experimental import pallas as pl
from jax.experimental.pallas import tpu as pltpu

D_MODEL = 1024
BRANCH_W = 512
N_BRANCH = 3
GMLP_CHUNK = 128
GMLP_GROUPS = 4
GMLP_GW = BRANCH_W // GMLP_GROUPS
CONV_W = 31
N_HEADS = 8
HEAD_DIM = 64
CHUNK = 64
WINDOW = 512
BAND = WINDOW + CHUNK
MAX_REL = 128
FFN_HIDDEN = 2816
IN_COLS = 7 * BRANCH_W + N_BRANCH * D_MODEL
ATTN_SCALE = HEAD_DIM ** -0.5
NEG_INF = -1e30
EPS = 1e-6

LANES = 128
SUBLANES = 8
N_PAIRS = N_HEADS // 2
KPAD = CHUNK
BAND_PAD = BAND + KPAD
CONV_HIST = CONV_W - 1
CONV_OFF = 32
VMEM_LIMIT = 56 * 1024 * 1024
MIXER_TILE = 512
FFN_TILE = 512
FFN_HC = 256

F32 = jnp.float32
BF16 = jnp.bfloat16
NT_DIMS = (((1,), (1,)), ((), ()))

GELU_A = 2.0 * math.sqrt(2.0 / math.pi)
GELU_B = GELU_A * 0.044715
LOG2E = math.log2(math.e)


def _rms(x, g):
    return x * lax.rsqrt(jnp.mean(x * x, axis=-1, keepdims=True) + EPS) * g


def _gelu(x):
    return x * jax.nn.sigmoid(x * (GELU_A + GELU_B * (x * x)))


def _mixer_kernel(*refs, T, has_cache, emit_vn, gc, carry):
    it = iter(refs)
    x_ref = next(it)
    if has_cache:
        ck_ref, cv_ref, cc_ref = next(it), next(it), next(it)
    gmix_ref, win_ref, bgate_ref = next(it), next(it), next(it)
    ggmlp_ref, ws_ref, gbias_ref = next(it), next(it), next(it)
    cdw_ref, cb_ref, gconv_ref = next(it), next(it), next(it)
    gqlo_ref, gqhi_ref, gk_ref, relb_ref = next(it), next(it), next(it), next(it)
    wbr_ref, wout_ref = next(it), next(it)
    xo_ref, kst_ref, vst_ref, cst_ref = next(it), next(it), next(it), next(it)
    if emit_vn:
        vn_ref = next(it)
    kwin, vwin, cbuf, ybuf = next(it), next(it), next(it), next(it)

    t = pl.program_id(1)
    W = BRANCH_W
    hist = KPAD + WINDOW
    n_chunks = T // CHUNK
    base = CONV_OFF - CONV_HIST

    @pl.when(t == 0)
    def _init():
        if has_cache:
            kwin[0:KPAD, :] = jnp.zeros((KPAD, W), BF16)
            vwin[0:KPAD, :] = jnp.zeros((KPAD, W), BF16)
            kwin[KPAD:hist, :] = ck_ref[0].astype(BF16)
            vwin[KPAD:hist, :] = cv_ref[0].astype(BF16)
            cbuf[0:CONV_OFF, :] = jnp.zeros((CONV_OFF, W), F32)
            cbuf[base:CONV_OFF, :] = cc_ref[0]
        else:
            kwin[0:hist, :] = jnp.zeros((hist, W), BF16)
            vwin[0:hist, :] = jnp.zeros((hist, W), BF16)
            cbuf[0:CONV_OFF, :] = jnp.zeros((CONV_OFF, W), F32)

    h = _rms(x_ref[0], gmix_ref[...]).astype(BF16)

    def sec(lo, width):
        return jnp.dot(h, win_ref[:, lo:lo + width], preferred_element_type=F32)

    def gate_pre(n):
        return sec(7 * W + n * D_MODEL, D_MODEL) + bgate_ref[:, n * D_MODEL:(n + 1) * D_MODEL]

    def conv_part(r):
        part = None
        for off in range(base, base + CONV_W):
            if off % SUBLANES == r:
                term = cbuf[off:off + T, :] * cdw_ref[off - base:off - base + 1, :]
                part = term if part is None else part + term
        return part

    low = lax.broadcasted_iota(jnp.int32, (T, LANES), 1) < HEAD_DIM

    def pair_unit(zp):
        sq = zp * zp
        s0 = jnp.sum(jnp.where(low, sq, 0.0), axis=1, keepdims=True)
        s1 = jnp.sum(jnp.where(low, 0.0, sq), axis=1, keepdims=True)
        r = jnp.where(low, lax.rsqrt(s0 * (1.0 / HEAD_DIM) + EPS), lax.rsqrt(s1 * (1.0 / HEAD_DIM) + EPS))
        return zp * r

    def scores(c):
        rows = slice(c * CHUNK, (c + 1) * CHUNK)
        out = []
        for p in range(N_PAIRS):
            q2 = jnp.concatenate([q_lo[p][rows], q_hi[p][rows]], axis=0)
            kb = kwin[c * CHUNK:c * CHUNK + BAND_PAD, p * LANES:(p + 1) * LANES]
            out.append(lax.dot_general(q2, kb, NT_DIMS, preferred_element_type=F32))
        return out

    col = lax.broadcasted_iota(jnp.int32, (2 * CHUNK, BAND_PAD), 1)
    low_c = lax.broadcasted_iota(jnp.int32, (CHUNK, LANES), 1) < HEAD_DIM

    def softmax(c, s_pairs):
        if not has_cache:
            valid = col >= (hist - t * T - c * CHUNK)
        out = []
        for p in range(N_PAIRS):
            s = s_pairs[p] + relb_ref[p]
            if not has_cache:
                s = jnp.where(valid, s, NEG_INF)
            e = jnp.exp2(s - jnp.max(s, axis=1, keepdims=True))
            out.append((e.astype(BF16), jnp.sum(e, axis=1, keepdims=True)))
        return out

    def pv(c, e_pairs):
        for p in range(N_PAIRS):
            sl = slice(p * LANES, (p + 1) * LANES)
            e, den = e_pairs[p]
            vb = vwin[c * CHUNK:c * CHUNK + BAND_PAD, sl]
            o2 = jnp.dot(e, vb, preferred_element_type=F32) / den
            ybuf[c * CHUNK:(c + 1) * CHUNK, sl] = jnp.where(low_c, o2[0:CHUNK], o2[CHUNK:2 * CHUNK]).astype(BF16)

    z_a = sec(2 * W, W)
    z_b = sec(3 * W, W)
    cbuf[CONV_OFF:CONV_OFF + T, :] = z_a * jax.nn.sigmoid(z_b)
    k = sec(5 * W, W)
    acc = conv_part(0) + conv_part(1)
    va = sec(6 * W, W)
    acc = acc + conv_part(2) + conv_part(3)
    for p in range(N_PAIRS):
        sl = slice(p * LANES, (p + 1) * LANES)
        k_n = pair_unit(k[:, sl]) * gk_ref[:, sl]
        kst_ref[0, :, sl] = k_n
        kwin[hist:hist + T, sl] = k_n.astype(BF16)
    q = sec(4 * W, W)
    acc = acc + conv_part(4) + conv_part(5)
    vst_ref[0] = va
    vwin[hist:hist + T, :] = va.astype(BF16)
    z_u = sec(0, W)
    acc = acc + conv_part(6) + conv_part(7)
    q_lo, q_hi = [], []
    for p in range(N_PAIRS):
        sl = slice(p * LANES, (p + 1) * LANES)
        q_unit = pair_unit(q[:, sl])
        q_lo.append((q_unit * gqlo_ref[:, sl]).astype(BF16))
        q_hi.append((q_unit * gqhi_ref[:, sl]).astype(BF16))
    z_v = sec(W, W)
    y_b = jax.nn.silu(_rms(acc + cb_ref[...], gconv_ref[...])).astype(BF16)
    cst_ref[0] = cbuf[base + T:base + T + CONV_HIST, :]
    cbuf[0:CONV_OFF, :] = cbuf[T:T + CONV_OFF, :]
    u = _gelu(z_u)
    g_pre0 = gate_pre(0)
    v_n = _rms(_gelu(z_v), ggmlp_ref[...])
    if emit_vn:
        vn_ref[0] = v_n
    v_nb = v_n.astype(BF16)
    g_pre1 = gate_pre(1)
    gate0 = jax.nn.sigmoid(g_pre0)
    g_pre2 = gate_pre(2)
    gate1 = jax.nn.sigmoid(g_pre1)

    s_pairs = scores(0)
    gate2 = jax.nn.sigmoid(g_pre2)
    r_i = lax.broadcasted_iota(jnp.int32, (GMLP_GROUPS, gc, gc), 1)
    c_i = lax.broadcasted_iota(jnp.int32, (GMLP_GROUPS, gc, gc), 2)
    wm = jnp.where(r_i >= c_i, ws_ref[:, 0:gc, 0:gc], 0.0).astype(BF16)
    row_blocks = []
    for n in range(T // gc):
        cols = [jnp.dot(wm[g], v_nb[n * gc:(n + 1) * gc, g * GMLP_GW:(g + 1) * GMLP_GW],
                        preferred_element_type=F32) for g in range(GMLP_GROUPS)]
        row_blocks.append(jnp.concatenate(cols, axis=1) + gbias_ref[0:gc, :])
    mixed = row_blocks[0] if len(row_blocks) == 1 else jnp.concatenate(row_blocks, axis=0)
    y_a = (u * mixed).astype(BF16)
    merged = None
    for c in range(n_chunks):
        s_next = scores(c + 1) if c + 1 < n_chunks else None
        e_pairs = softmax(c, s_pairs)
        if c == 0:
            merged = gate0 * jnp.dot(y_a, wbr_ref[0], preferred_element_type=F32)
        if c == min(1, n_chunks - 1):
            merged = merged + gate1 * jnp.dot(y_b, wbr_ref[1], preferred_element_type=F32)
        pv(c, e_pairs)
        s_pairs = s_next

    merged = merged + gate2 * jnp.dot(ybuf[...], wbr_ref[2], preferred_element_type=F32)
    xo_ref[0] = x_ref[0] + jnp.dot(merged.astype(BF16), wout_ref[...], preferred_element_type=F32)

    if carry:
        kwin[KPAD:hist, :] = kwin[KPAD + T:hist + T, :]
        vwin[KPAD:hist, :] = vwin[KPAD + T:hist + T, :]


def _layer_spec(a, layer):
    zeros = (0,) * (a.ndim - 1)
    return pl.BlockSpec((None,) + a.shape[1:], lambda *g: (layer,) + zeros, pipeline_mode=pl.Buffered(1))


def _mixer(x, cache, p, layer, *, T, emit_vn):
    B, L, _ = x.shape
    W = BRANCH_W
    nT = L // T
    keep = min(WINDOW, L)
    nkeep = keep // T
    gc = min(L, GMLP_CHUNK)
    has_cache = cache is not None
    assert L % T == 0 and T % CHUNK == 0 and T % gc == 0 and keep % T == 0 and T <= WINDOW

    tile = lambda b, t: (b, t, 0)
    per_b = lambda b, t: (b, 0, 0)
    state = lambda b, t: (b, jnp.maximum(t - (nT - nkeep), 0), 0)

    args = [x]
    in_specs = [pl.BlockSpec((1, T, D_MODEL), tile)]
    if has_cache:
        ck, cv, cc = cache
        args += [ck.reshape(B, WINDOW, W), cv.reshape(B, WINDOW, W), cc]
        in_specs += [pl.BlockSpec((1, WINDOW, W), per_b), pl.BlockSpec((1, WINDOW, W), per_b),
                     pl.BlockSpec((1, CONV_HIST, W), per_b)]
    params = [p[name] for name in ("g_mix", "w_in", "b_gate", "g_gmlp", "ws", "gbias", "conv_dw", "conv_b",
                                   "g_conv", "gq_lo", "gq_hi", "g_k", "relb", "w_branch", "w_out")]
    args += params
    in_specs += [_layer_spec(a, layer) for a in params]

    out_shape = [jax.ShapeDtypeStruct((B, L, D_MODEL), F32),
                 jax.ShapeDtypeStruct((B, keep, W), F32),
                 jax.ShapeDtypeStruct((B, keep, W), F32),
                 jax.ShapeDtypeStruct((B, CONV_HIST, W), F32)]
    out_specs = [pl.BlockSpec((1, T, D_MODEL), tile),
                 pl.BlockSpec((1, T, W), state),
                 pl.BlockSpec((1, T, W), state),
                 pl.BlockSpec((1, CONV_HIST, W), per_b)]
    if emit_vn:
        out_shape.append(jax.ShapeDtypeStruct((B, L, W), F32))
        out_specs.append(pl.BlockSpec((1, T, W), tile))

    win_rows = KPAD + WINDOW + T
    return pl.pallas_call(
        functools.partial(_mixer_kernel, T=T, has_cache=has_cache, emit_vn=emit_vn, gc=gc, carry=nT > 1),
        grid=(B, nT),
        in_specs=in_specs,
        out_specs=out_specs,
        out_shape=out_shape,
        scratch_shapes=[pltpu.VMEM((win_rows, W), BF16), pltpu.VMEM((win_rows, W), BF16),
                        pltpu.VMEM((CONV_OFF + T, W), F32), pltpu.VMEM((T, W), BF16)],
        compiler_params=pltpu.CompilerParams(dimension_semantics=("arbitrary", "arbitrary"),
                                             vmem_limit_bytes=VMEM_LIMIT),
        name="mixer_cache" if has_cache else "mixer_prompt",
    )(*args)


def _ffn_kernel(x_ref, g_ref, wgu_ref, wd_ref, o_ref):
    x = x_ref[...]
    h2 = _rms(x, g_ref[...]).astype(BF16)
    acc = x
    for j in range(FFN_HIDDEN // FFN_HC):
        lo = j * FFN_HC
        g = jnp.dot(h2, wgu_ref[:, lo:lo + FFN_HC], preferred_element_type=F32)
        u = jnp.dot(h2, wgu_ref[:, FFN_HIDDEN + lo:FFN_HIDDEN + lo + FFN_HC], preferred_element_type=F32)
        a = (jax.nn.silu(g) * u).astype(BF16)
        acc = acc + jnp.dot(a, wd_ref[lo:lo + FFN_HC, :], preferred_element_type=F32)
    o_ref[...] = acc


def _ffn(x, p, layer, *, T):
    B, L, _ = x.shape
    rows = B * L
    assert rows % T == 0 and FFN_HIDDEN % FFN_HC == 0
    params = [p["g_ffn"], p["w_gate_up"], p["w_down"]]
    out = pl.pallas_call(
        _ffn_kernel,
        grid=(rows // T,),
        in_specs=[pl.BlockSpec((T, D_MODEL), lambda i: (i, 0))] + [_layer_spec(a, layer) for a in params],
        out_specs=pl.BlockSpec((T, D_MODEL), lambda i: (i, 0)),
        out_shape=jax.ShapeDtypeStruct((rows, D_MODEL), F32),
        compiler_params=pltpu.CompilerParams(dimension_semantics=("arbitrary",),
                                             vmem_limit_bytes=VMEM_LIMIT),
        name="ffn",
    )(x.reshape(rows, D_MODEL), *params)
    return out.reshape(B, L, D_MODEL)


def _rel_bias_table(rel_bias):
    n_heads = rel_bias.shape[0]
    n_far = BAND_PAD - MAX_REL
    period = BAND_PAD + CHUNK
    far = jnp.broadcast_to(rel_bias[:, 2 * MAX_REL:2 * MAX_REL + 1], (n_heads, n_far))
    near = jnp.flip(rel_bias[:, 2 * MAX_REL - (period - n_far):2 * MAX_REL], axis=1)
    g = jnp.concatenate([far, near], axis=1) * LOG2E
    skew = jnp.tile(g, (1, CHUNK))[:, :CHUNK * (period - 1)].reshape(n_heads, CHUNK, period - 1)
    tbl = skew[:, :, CHUNK - 1:CHUNK - 1 + BAND_PAD]
    tbl = jnp.where(jnp.arange(BAND_PAD)[None, None, :] >= KPAD, tbl, NEG_INF)
    return tbl.reshape(n_heads // 2, 2 * CHUNK, BAND_PAD).astype(F32)


def kernel(x_prompt, x_sample, cache_attn_k, cache_attn_v, cache_conv, norm_mix_g, w_in, b_gate, gmlp_norm_g, gmlp_ws, gmlp_bs, conv_dw, conv_b, conv_norm_g, q_norm_g, k_norm_g, rel_bias, w_branch, w_out, norm_ffn_g, w_gate_up, w_down):
    depth = w_in.shape[0]
    xp, xs = x_prompt, x_sample
    Bp, Bs, Ls = xp.shape[0], xs.shape[0], xs.shape[1]
    rows = lambda a: a.reshape(depth, 1, -1)
    lane_low = (jnp.arange(BRANCH_W) % LANES) < HEAD_DIM
    gq = jnp.tile(q_norm_g, (1, N_HEADS)) * (ATTN_SCALE * LOG2E)
    p = {
        "w_in": w_in.astype(BF16), "w_branch": w_branch.astype(BF16), "w_out": w_out.astype(BF16),
        "w_gate_up": w_gate_up.astype(BF16), "w_down": w_down.astype(BF16),
        "g_mix": rows(norm_mix_g), "b_gate": rows(b_gate), "g_gmlp": rows(gmlp_norm_g), "ws": gmlp_ws,
        "gbias": jnp.repeat(jnp.swapaxes(gmlp_bs, 1, 2), GMLP_GW, axis=2),
        "conv_dw": conv_dw, "conv_b": rows(conv_b), "g_conv": rows(conv_norm_g),
        "gq_lo": rows(jnp.where(lane_low, gq, 0.0)), "gq_hi": rows(jnp.where(lane_low, 0.0, gq)),
        "g_k": rows(jnp.tile(k_norm_g, (1, N_HEADS))),
        "relb": _rel_bias_table(rel_bias.reshape(depth * N_HEADS, -1)).reshape(depth, N_PAIRS, 2 * CHUNK, BAND_PAD),
        "g_ffn": rows(norm_ffn_g),
    }
    kp_l, vp_l, cp_l, ks_l, vs_l, cs_l, gs_l = [], [], [], [], [], [], []
    for l in range(depth):
        xp, kp, vp, cp = _mixer(xp, None, p, l, T=MIXER_TILE, emit_vn=False)
        xp = _ffn(xp, p, l, T=FFN_TILE)
        xs, ks, vs, cs, gs = _mixer(xs, (cache_attn_k[l], cache_attn_v[l], cache_conv[l]), p, l, T=Ls, emit_vn=True)
        xs = _ffn(xs, p, l, T=Bs * Ls)
        kp_l.append(kp.reshape(Bp, -1, N_HEADS, HEAD_DIM))
        vp_l.append(vp.reshape(Bp, -1, N_HEADS, HEAD_DIM))
        cp_l.append(cp)
        ks_l.append(ks.reshape(Bs, Ls, N_HEADS, HEAD_DIM))
        vs_l.append(vs.reshape(Bs, Ls, N_HEADS, HEAD_DIM))
        cs_l.append(cs)
        gs_l.append(gs)
    return (xp, xs, jnp.stack(kp_l), jnp.stack(vp_l), jnp.stack(cp_l),
            jnp.stack(ks_l), jnp.stack(vs_l), jnp.stack(cs_l), jnp.stack(gs_l))
```

```python
import functools
import math

import jax
import jax.numpy as jnp
from jax import lax
from jax.experimental import pallas as pl
from jax.experimental.pallas import tpu as pltpu

D_MODEL = 1024
BRANCH_W = 512
N_BRANCH = 3
GMLP_CHUNK = 128
GMLP_GROUPS = 4
GMLP_GW = BRANCH_W // GMLP_GROUPS
CONV_W = 31
N_HEADS = 8
HEAD_DIM = 64
CHUNK = 64
WINDOW = 512
BAND = WINDOW + CHUNK
MAX_REL = 128
FFN_HIDDEN = 2816
IN_COLS = 7 * BRANCH_W + N_BRANCH * D_MODEL
ATTN_SCALE = HEAD_DIM ** -0.5
NEG_INF = -1e30
EPS = 1e-6

LANES = 128
SUBLANES = 8
N_PAIRS = N_HEADS // 2
KPAD = CHUNK
BAND_PAD = BAND + KPAD
CONV_HIST = CONV_W - 1
CONV_OFF = 32
VMEM_LIMIT = 56 * 1024 * 1024
MIXER_TILE = 512
SAMPLE_SEQS = 2
FFN_TILE = 512
FFN_HC = 256

F32 = jnp.float32
BF16 = jnp.bfloat16
NT_DIMS = (((1,), (1,)), ((), ()))

GELU_A = 2.0 * math.sqrt(2.0 / math.pi)
GELU_B = GELU_A * 0.044715
LOG2E = math.log2(math.e)


def _rms(x, g):
    return x * lax.rsqrt(jnp.mean(x * x, axis=-1, keepdims=True) + EPS) * g


def _gelu(x):
    return x * jax.nn.sigmoid(x * (GELU_A + GELU_B * (x * x)))


def _mixer_kernel(*refs, n_seq, Ts, has_cache, emit_vn, gc, carry):
    it = iter(refs)
    x_ref = next(it)
    if has_cache:
        ck_ref, cv_ref, cc_ref = next(it), next(it), next(it)
    gmix_ref, win_ref, bgate_ref = next(it), next(it), next(it)
    ggmlp_ref, ws_ref, gbias_ref = next(it), next(it), next(it)
    cdw_ref, cb_ref, gconv_ref = next(it), next(it), next(it)
    gqlo_ref, gqhi_ref, gk_ref, relb_ref = next(it), next(it), next(it), next(it)
    wbr_ref, wout_ref = next(it), next(it)
    xo_ref, kst_ref, vst_ref, cst_ref = next(it), next(it), next(it), next(it)
    if emit_vn:
        vn_ref = next(it)
    kwin, vwin, cbuf, ybuf = next(it), next(it), next(it), next(it)

    t = pl.program_id(1)
    W = BRANCH_W
    T = n_seq * Ts
    hist = KPAD + WINDOW
    seq_chunks = Ts // CHUNK
    n_chunks = n_seq * seq_chunks
    base = CONV_OFF - CONV_HIST
    seq_rows = [slice(s * Ts, (s + 1) * Ts) for s in range(n_seq)]

    def rows_of(ref):
        return ref[0] if n_seq == 1 else jnp.concatenate([ref[s] for s in range(n_seq)], axis=0)

    @pl.when(t == 0)
    def _init():
        for s in range(n_seq):
            if has_cache:
                kwin[s, 0:KPAD, :] = jnp.zeros((KPAD, W), BF16)
                vwin[s, 0:KPAD, :] = jnp.zeros((KPAD, W), BF16)
                kwin[s, KPAD:hist, :] = ck_ref[s].astype(BF16)
                vwin[s, KPAD:hist, :] = cv_ref[s].astype(BF16)
                cbuf[s, 0:CONV_OFF, :] = jnp.zeros((CONV_OFF, W), F32)
                cbuf[s, base:CONV_OFF, :] = cc_ref[s]
            else:
                kwin[s, 0:hist, :] = jnp.zeros((hist, W), BF16)
                vwin[s, 0:hist, :] = jnp.zeros((hist, W), BF16)
                cbuf[s, 0:CONV_OFF, :] = jnp.zeros((CONV_OFF, W), F32)

    h = _rms(rows_of(x_ref), gmix_ref[...]).astype(BF16)

    def sec(lo, width):
        return jnp.dot(h, win_ref[:, lo:lo + width], preferred_element_type=F32)

    def gate_pre(n):
        return sec(7 * W + n * D_MODEL, D_MODEL) + bgate_ref[:, n * D_MODEL:(n + 1) * D_MODEL]

    conv_acc = [None] * n_seq

    def add_conv_taps(*residues):
        for s in range(n_seq):
            for r in residues:
                part = None
                for off in range(base, base + CONV_W):
                    if off % SUBLANES == r:
                        term = cbuf[s, off:off + Ts, :] * cdw_ref[off - base:off - base + 1, :]
                        part = term if part is None else part + term
                conv_acc[s] = part if conv_acc[s] is None else conv_acc[s] + part

    low = lax.broadcasted_iota(jnp.int32, (T, LANES), 1) < HEAD_DIM

    def pair_unit(zp):
        sq = zp * zp
        s0 = jnp.sum(jnp.where(low, sq, 0.0), axis=1, keepdims=True)
        s1 = jnp.sum(jnp.where(low, 0.0, sq), axis=1, keepdims=True)
        r = jnp.where(low, lax.rsqrt(s0 * (1.0 / HEAD_DIM) + EPS), lax.rsqrt(s1 * (1.0 / HEAD_DIM) + EPS))
        return zp * r

    def band(win, c, p):
        s, lo = c // seq_chunks, (c % seq_chunks) * CHUNK
        return win[s, lo:lo + BAND_PAD, p * LANES:(p + 1) * LANES]

    def scores(c):
        rows = slice(c * CHUNK, (c + 1) * CHUNK)
        out = []
        for p in range(N_PAIRS):
            q2 = jnp.concatenate([q_lo[p][rows], q_hi[p][rows]], axis=0)
            kb = band(kwin, c, p)
            out.append(lax.dot_general(q2, kb, NT_DIMS, preferred_element_type=F32))
        return out

    col = lax.broadcasted_iota(jnp.int32, (2 * CHUNK, BAND_PAD), 1)
    low_c = lax.broadcasted_iota(jnp.int32, (CHUNK, LANES), 1) < HEAD_DIM

    def softmax(c, s_pairs):
        if not has_cache:
            valid = col >= (hist - t * Ts - (c % seq_chunks) * CHUNK)
        out = []
        for p in range(N_PAIRS):
            s = s_pairs[p] + relb_ref[p]
            if not has_cache:
                s = jnp.where(valid, s, NEG_INF)
            e = jnp.exp2(s - jnp.max(s, axis=1, keepdims=True))
            out.append((e.astype(BF16), jnp.sum(e, axis=1, keepdims=True)))
        return out

    def pv(c, e_pairs):
        for p in range(N_PAIRS):
            sl = slice(p * LANES, (p + 1) * LANES)
            e, den = e_pairs[p]
            vb = band(vwin, c, p)
            o2 = jnp.dot(e, vb, preferred_element_type=F32) / den
            ybuf[c * CHUNK:(c + 1) * CHUNK, sl] = jnp.where(low_c, o2[0:CHUNK], o2[CHUNK:2 * CHUNK]).astype(BF16)

    z_a = sec(2 * W, W)
    z_b = sec(3 * W, W)
    glu = z_a * jax.nn.sigmoid(z_b)
    for s in range(n_seq):
        cbuf[s, CONV_OFF:CONV_OFF + Ts, :] = glu[seq_rows[s]]
    k = sec(5 * W, W)
    add_conv_taps(0, 1)
    va = sec(6 * W, W)
    add_conv_taps(2, 3)
    for p in range(N_PAIRS):
        sl = slice(p * LANES, (p + 1) * LANES)
        k_n = pair_unit(k[:, sl]) * gk_ref[:, sl]
        for s in range(n_seq):
            kst_ref[s, :, sl] = k_n[seq_rows[s]]
            kwin[s, hist:hist + Ts, sl] = k_n[seq_rows[s]].astype(BF16)
    q = sec(4 * W, W)
    add_conv_taps(4, 5)
    for s in range(n_seq):
        vst_ref[s] = va[seq_rows[s]]
        vwin[s, hist:hist + Ts, :] = va[seq_rows[s]].astype(BF16)
    z_u = sec(0, W)
    add_conv_taps(6, 7)
    q_lo, q_hi = [], []
    for p in range(N_PAIRS):
        sl = slice(p * LANES, (p + 1) * LANES)
        q_unit = pair_unit(q[:, sl])
        q_lo.append((q_unit * gqlo_ref[:, sl]).astype(BF16))
        q_hi.append((q_unit * gqhi_ref[:, sl]).astype(BF16))
    z_v = sec(W, W)
    conv = conv_acc[0] if n_seq == 1 else jnp.concatenate(conv_acc, axis=0)
    y_b = jax.nn.silu(_rms(conv + cb_ref[...], gconv_ref[...])).astype(BF16)
    for s in range(n_seq):
        cst_ref[s] = cbuf[s, base + Ts:base + Ts + CONV_HIST, :]
        cbuf[s, 0:CONV_OFF, :] = cbuf[s, Ts:Ts + CONV_OFF, :]
    u = _gelu(z_u)
    g_pre0 = gate_pre(0)
    v_n = _rms(_gelu(z_v), ggmlp_ref[...])
    if emit_vn:
        for s in range(n_seq):
            vn_ref[s] = v_n[seq_rows[s]]
    v_nb = v_n.astype(BF16)
    g_pre1 = gate_pre(1)
    gate0 = jax.nn.sigmoid(g_pre0)
    g_pre2 = gate_pre(2)
    gate1 = jax.nn.sigmoid(g_pre1)

    s_pairs = scores(0)
    gate2 = jax.nn.sigmoid(g_pre2)
    r_i = lax.broadcasted_iota(jnp.int32, (GMLP_GROUPS, gc, gc), 1)
    c_i = lax.broadcasted_iota(jnp.int32, (GMLP_GROUPS, gc, gc), 2)
    wm = jnp.where(r_i >= c_i, ws_ref[:, 0:gc, 0:gc], 0.0).astype(BF16)
    row_blocks = []
    for n in range(T // gc):
        cols = [jnp.dot(wm[g], v_nb[n * gc:(n + 1) * gc, g * GMLP_GW:(g + 1) * GMLP_GW],
                        preferred_element_type=F32) for g in range(GMLP_GROUPS)]
        row_blocks.append(jnp.concatenate(cols, axis=1) + gbias_ref[0:gc, :])
    mixed = row_blocks[0] if len(row_blocks) == 1 else jnp.concatenate(row_blocks, axis=0)
    y_a = (u * mixed).astype(BF16)
    merged = None
    for c in range(n_chunks):
        s_next = scores(c + 1) if c + 1 < n_chunks else None
        e_pairs = softmax(c, s_pairs)
        if c == 0:
            merged = gate0 * jnp.dot(y_a, wbr_ref[0], preferred_element_type=F32)
        if c == min(1, n_chunks - 1):
            merged = merged + gate1 * jnp.dot(y_b, wbr_ref[1], preferred_element_type=F32)
        pv(c, e_pairs)
        s_pairs = s_next

    merged = merged + gate2 * jnp.dot(ybuf[...], wbr_ref[2], preferred_element_type=F32)
    x_out = rows_of(x_ref) + jnp.dot(merged.astype(BF16), wout_ref[...], preferred_element_type=F32)
    for s in range(n_seq):
        xo_ref[s] = x_out[seq_rows[s]]

    if carry:
        for s in range(n_seq):
            kwin[s, KPAD:hist, :] = kwin[s, KPAD + Ts:hist + Ts, :]
            vwin[s, KPAD:hist, :] = vwin[s, KPAD + Ts:hist + Ts, :]


def _layer_spec(a, layer):
    zeros = (0,) * (a.ndim - 1)
    return pl.BlockSpec((None,) + a.shape[1:], lambda *g: (layer,) + zeros, pipeline_mode=pl.Buffered(1))


def _mixer(x, cache, p, layer, *, Ts, n_seq, emit_vn):
    B, L, _ = x.shape
    W = BRANCH_W
    nT = L // Ts
    keep = min(WINDOW, L)
    nkeep = keep // Ts
    gc = min(L, GMLP_CHUNK)
    has_cache = cache is not None
    assert L % Ts == 0 and Ts % CHUNK == 0 and Ts % gc == 0 and keep % Ts == 0 and Ts <= WINDOW and B % n_seq == 0

    tile = lambda b, t: (b, t, 0)
    per_b = lambda b, t: (b, 0, 0)
    state = lambda b, t: (b, jnp.maximum(t - (nT - nkeep), 0), 0)

    args = [x]
    in_specs = [pl.BlockSpec((n_seq, Ts, D_MODEL), tile)]
    if has_cache:
        ck, cv, cc = cache
        args += [ck.reshape(B, WINDOW, W), cv.reshape(B, WINDOW, W), cc]
        in_specs += [pl.BlockSpec((n_seq, WINDOW, W), per_b), pl.BlockSpec((n_seq, WINDOW, W), per_b),
                     pl.BlockSpec((n_seq, CONV_HIST, W), per_b)]
    params = [p[name] for name in ("g_mix", "w_in", "b_gate", "g_gmlp", "ws", "gbias", "conv_dw", "conv_b",
                                   "g_conv", "gq_lo", "gq_hi", "g_k", "relb", "w_branch", "w_out")]
    args += params
    in_specs += [_layer_spec(a, layer) for a in params]

    out_shape = [jax.ShapeDtypeStruct((B, L, D_MODEL), F32),
                 jax.ShapeDtypeStruct((B, keep, W), F32),
                 jax.ShapeDtypeStruct((B, keep, W), F32),
                 jax.ShapeDtypeStruct((B, CONV_HIST, W), F32)]
    out_specs = [pl.BlockSpec((n_seq, Ts, D_MODEL), tile),
                 pl.BlockSpec((n_seq, Ts, W), state),
                 pl.BlockSpec((n_seq, Ts, W), state),
                 pl.BlockSpec((n_seq, CONV_HIST, W), per_b)]
    if emit_vn:
        out_shape.append(jax.ShapeDtypeStruct((B, L, W), F32))
        out_specs.append(pl.BlockSpec((n_seq, Ts, W), tile))

    win_rows = KPAD + WINDOW + Ts
    return pl.pallas_call(
        functools.partial(_mixer_kernel, n_seq=n_seq, Ts=Ts, has_cache=has_cache, emit_vn=emit_vn, gc=gc,
                          carry=nT > 1),
        grid=(B // n_seq, nT),
        in_specs=in_specs,
        out_specs=out_specs,
        out_shape=out_shape,
        scratch_shapes=[pltpu.VMEM((n_seq, win_rows, W), BF16), pltpu.VMEM((n_seq, win_rows, W), BF16),
                        pltpu.VMEM((n_seq, CONV_OFF + Ts, W), F32), pltpu.VMEM((n_seq * Ts, W), BF16)],
        compiler_params=pltpu.CompilerParams(dimension_semantics=("arbitrary", "arbitrary"),
                                             vmem_limit_bytes=VMEM_LIMIT),
        name="mixer_cache" if has_cache else "mixer_prompt",
    )(*args)


def _ffn_kernel(x_ref, g_ref, wgu_ref, wd_ref, o_ref):
    x = x_ref[...]
    h2 = _rms(x, g_ref[...]).astype(BF16)
    acc = x
    for j in range(FFN_HIDDEN // FFN_HC):
        lo = j * FFN_HC
        g = jnp.dot(h2, wgu_ref[:, lo:lo + FFN_HC], preferred_element_type=F32)
        u = jnp.dot(h2, wgu_ref[:, FFN_HIDDEN + lo:FFN_HIDDEN + lo + FFN_HC], preferred_element_type=F32)
        a = (jax.nn.silu(g) * u).astype(BF16)
        acc = acc + jnp.dot(a, wd_ref[lo:lo + FFN_HC, :], preferred_element_type=F32)
    o_ref[...] = acc


def _ffn(x, p, layer, *, T):
    B, L, _ = x.shape
    rows = B * L
    assert rows % T == 0 and FFN_HIDDEN % FFN_HC == 0
    params = [p["g_ffn"], p["w_gate_up"], p["w_down"]]
    out = pl.pallas_call(
        _ffn_kernel,
        grid=(rows // T,),
        in_specs=[pl.BlockSpec((T, D_MODEL), lambda i: (i, 0))] + [_layer_spec(a, layer) for a in params],
        out_specs=pl.BlockSpec((T, D_MODEL), lambda i: (i, 0)),
        out_shape=jax.ShapeDtypeStruct((rows, D_MODEL), F32),
        compiler_params=pltpu.CompilerParams(dimension_semantics=("arbitrary",),
                                             vmem_limit_bytes=VMEM_LIMIT),
        name="ffn",
    )(x.reshape(rows, D_MODEL), *params)
    return out.reshape(B, L, D_MODEL)


def _rel_bias_table(rel_bias):
    n_heads = rel_bias.shape[0]
    n_far = BAND_PAD - MAX_REL
    period = BAND_PAD + CHUNK
    far = jnp.broadcast_to(rel_bias[:, 2 * MAX_REL:2 * MAX_REL + 1], (n_heads, n_far))
    near = jnp.flip(rel_bias[:, 2 * MAX_REL - (period - n_far):2 * MAX_REL], axis=1)
    g = jnp.concatenate([far, near], axis=1) * LOG2E
    skew = jnp.tile(g, (1, CHUNK))[:, :CHUNK * (period - 1)].reshape(n_heads, CHUNK, period - 1)
    tbl = skew[:, :, CHUNK - 1:CHUNK - 1 + BAND_PAD]
    tbl = jnp.where(jnp.arange(BAND_PAD)[None, None, :] >= KPAD, tbl, NEG_INF)
    return tbl.reshape(n_heads // 2, 2 * CHUNK, BAND_PAD).astype(F32)


def kernel(x_prompt, x_sample, cache_attn_k, cache_attn_v, cache_conv, norm_mix_g, w_in, b_gate, gmlp_norm_g, gmlp_ws, gmlp_bs, conv_dw, conv_b, conv_norm_g, q_norm_g, k_norm_g, rel_bias, w_branch, w_out, norm_ffn_g, w_gate_up, w_down):
    depth = w_in.shape[0]
    xp, xs = x_prompt, x_sample
    Bp, Bs, Ls = xp.shape[0], xs.shape[0], xs.shape[1]
    rows = lambda a: a.reshape(depth, 1, -1)
    lane_low = (jnp.arange(BRANCH_W) % LANES) < HEAD_DIM
    gq = jnp.tile(q_norm_g, (1, N_HEADS)) * (ATTN_SCALE * LOG2E)
    p = {
        "w_in": w_in.astype(BF16), "w_branch": w_branch.astype(BF16), "w_out": w_out.astype(BF16),
        "w_gate_up": w_gate_up.astype(BF16), "w_down": w_down.astype(BF16),
        "g_mix": rows(norm_mix_g), "b_gate": rows(b_gate), "g_gmlp": rows(gmlp_norm_g), "ws": gmlp_ws,
        "gbias": jnp.repeat(jnp.swapaxes(gmlp_bs, 1, 2), GMLP_GW, axis=2),
        "conv_dw": conv_dw, "conv_b": rows(conv_b), "g_conv": rows(conv_norm_g),
        "gq_lo": rows(jnp.where(lane_low, gq, 0.0)), "gq_hi": rows(jnp.where(lane_low, 0.0, gq)),
        "g_k": rows(jnp.tile(k_norm_g, (1, N_HEADS))),
        "relb": _rel_bias_table(rel_bias.reshape(depth * N_HEADS, -1)).reshape(depth, N_PAIRS, 2 * CHUNK, BAND_PAD),
        "g_ffn": rows(norm_ffn_g),
    }
    kp_l, vp_l, cp_l, ks_l, vs_l, cs_l, gs_l = [], [], [], [], [], [], []
    for l in range(depth):
        xp, kp, vp, cp = _mixer(xp, None, p, l, Ts=MIXER_TILE, n_seq=1, emit_vn=False)
        xp = _ffn(xp, p, l, T=FFN_TILE)
        xs, ks, vs, cs, gs = _mixer(xs, (cache_attn_k[l], cache_attn_v[l], cache_conv[l]), p, l, Ts=Ls,
                                    n_seq=SAMPLE_SEQS, emit_vn=True)
        xs = _ffn(xs, p, l, T=Bs * Ls)
        kp_l.append(kp.reshape(Bp, -1, N_HEADS, HEAD_DIM))
        vp_l.append(vp.reshape(Bp, -1, N_HEADS, HEAD_DIM))
        cp_l.append(cp)
        ks_l.append(ks.reshape(Bs, Ls, N_HEADS, HEAD_DIM))
        vs_l.append(vs.reshape(Bs, Ls, N_HEADS, HEAD_DIM))
        cs_l.append(cs)
        gs_l.append(gs)
    return (xp, xs, jnp.stack(kp_l), jnp.stack(vp_l), jnp.stack(cp_l),
            jnp.stack(ks_l), jnp.stack(vs_l), jnp.stack(cs_l), jnp.stack(gs_l))
```

```python
import functools
import math

import jax
import jax.numpy as jnp
from jax import lax
from jax.experimental import pallas as pl
from jax.experimental.pallas import tpu as pltpu

D_MODEL = 1024
BRANCH_W = 512
N_BRANCH = 3
GMLP_CHUNK = 128
GMLP_GROUPS = 4
GMLP_GW = BRANCH_W // GMLP_GROUPS
CONV_W = 31
N_HEADS = 8
HEAD_DIM = 64
CHUNK = 64
WINDOW = 512
BAND = WINDOW + CHUNK
MAX_REL = 128
FFN_HIDDEN = 2816
IN_COLS = 7 * BRANCH_W + N_BRANCH * D_MODEL
ATTN_SCALE = HEAD_DIM ** -0.5
NEG_INF = -1e30
EPS = 1e-6

LANES = 128
SUBLANES = 8
N_PAIRS = N_HEADS // 2
KPAD = CHUNK
BAND_PAD = BAND + KPAD
CONV_HIST = CONV_W - 1
CONV_OFF = 32
VMEM_LIMIT = 56 * 1024 * 1024
MIXER_TILE = 512
SAMPLE_SEQS = 2
FFN_TILE = 1024
FFN_HC = 256

F32 = jnp.float32
BF16 = jnp.bfloat16
NT_DIMS = (((1,), (1,)), ((), ()))

GELU_A = 2.0 * math.sqrt(2.0 / math.pi)
GELU_B = GELU_A * 0.044715
LOG2E = math.log2(math.e)


def _rms(x, g):
    return x * lax.rsqrt(jnp.mean(x * x, axis=-1, keepdims=True) + EPS) * g


def _gelu(x):
    return x * jax.nn.sigmoid(x * (GELU_A + GELU_B * (x * x)))


def _mixer_kernel(*refs, n_seq, Ts, has_cache, emit_vn, gc, carry):
    it = iter(refs)
    x_ref = next(it)
    if has_cache:
        ck_ref, cv_ref, cc_ref = next(it), next(it), next(it)
    gmix_ref, win_ref, bgate_ref = next(it), next(it), next(it)
    ggmlp_ref, ws_ref, gbias_ref = next(it), next(it), next(it)
    cdw_ref, cb_ref, gconv_ref = next(it), next(it), next(it)
    gqlo_ref, gqhi_ref, gk_ref, relb_ref = next(it), next(it), next(it), next(it)
    wbr_ref, wout_ref = next(it), next(it)
    xo_ref, kst_ref, vst_ref, cst_ref = next(it), next(it), next(it), next(it)
    if emit_vn:
        vn_ref = next(it)
    kwin, vwin, cbuf, ybuf = next(it), next(it), next(it), next(it)

    t = pl.program_id(1)
    W = BRANCH_W
    T = n_seq * Ts
    hist = KPAD + WINDOW
    seq_chunks = Ts // CHUNK
    n_chunks = n_seq * seq_chunks
    base = CONV_OFF - CONV_HIST
    seq_rows = [slice(s * Ts, (s + 1) * Ts) for s in range(n_seq)]

    def rows_of(ref):
        return ref[0] if n_seq == 1 else jnp.concatenate([ref[s] for s in range(n_seq)], axis=0)

    @pl.when(t == 0)
    def _init():
        for s in range(n_seq):
            if has_cache:
                kwin[s, 0:KPAD, :] = jnp.zeros((KPAD, W), BF16)
                vwin[s, 0:KPAD, :] = jnp.zeros((KPAD, W), BF16)
                kwin[s, KPAD:hist, :] = ck_ref[s].astype(BF16)
                vwin[s, KPAD:hist, :] = cv_ref[s].astype(BF16)
                cbuf[s, 0:CONV_OFF, :] = jnp.zeros((CONV_OFF, W), F32)
                cbuf[s, base:CONV_OFF, :] = cc_ref[s]
            else:
                kwin[s, 0:hist, :] = jnp.zeros((hist, W), BF16)
                vwin[s, 0:hist, :] = jnp.zeros((hist, W), BF16)
                cbuf[s, 0:CONV_OFF, :] = jnp.zeros((CONV_OFF, W), F32)

    h = _rms(rows_of(x_ref), gmix_ref[...]).astype(BF16)

    def sec(lo, width):
        return jnp.dot(h, win_ref[:, lo:lo + width], preferred_element_type=F32)

    def gate_pre(n):
        return sec(7 * W + n * D_MODEL, D_MODEL) + bgate_ref[:, n * D_MODEL:(n + 1) * D_MODEL]

    conv_acc = [None] * n_seq

    def add_conv_taps(*residues):
        for s in range(n_seq):
            for r in residues:
                part = None
                for off in range(base, base + CONV_W):
                    if off % SUBLANES == r:
                        term = cbuf[s, off:off + Ts, :] * cdw_ref[off - base:off - base + 1, :]
                        part = term if part is None else part + term
                conv_acc[s] = part if conv_acc[s] is None else conv_acc[s] + part

    low = lax.broadcasted_iota(jnp.int32, (T, LANES), 1) < HEAD_DIM

    def pair_unit(zp):
        sq = zp * zp
        s0 = jnp.sum(jnp.where(low, sq, 0.0), axis=1, keepdims=True)
        s1 = jnp.sum(jnp.where(low, 0.0, sq), axis=1, keepdims=True)
        r = jnp.where(low, lax.rsqrt(s0 * (1.0 / HEAD_DIM) + EPS), lax.rsqrt(s1 * (1.0 / HEAD_DIM) + EPS))
        return zp * r

    def band(win, c, p):
        s, lo = c // seq_chunks, (c % seq_chunks) * CHUNK
        return win[s, lo:lo + BAND_PAD, p * LANES:(p + 1) * LANES]

    def scores(c):
        rows = slice(c * CHUNK, (c + 1) * CHUNK)
        out = []
        for p in range(N_PAIRS):
            q2 = jnp.concatenate([q_lo[p][rows], q_hi[p][rows]], axis=0)
            kb = band(kwin, c, p)
            out.append(lax.dot_general(q2, kb, NT_DIMS, preferred_element_type=F32))
        return out

    col = lax.broadcasted_iota(jnp.int32, (2 * CHUNK, BAND_PAD), 1)
    low_c = lax.broadcasted_iota(jnp.int32, (CHUNK, LANES), 1) < HEAD_DIM

    def softmax(c, s_pairs):
        if not has_cache:
            valid = col >= (hist - t * Ts - (c % seq_chunks) * CHUNK)
        out = []
        for p in range(N_PAIRS):
            s = s_pairs[p] + relb_ref[p]
            if not has_cache:
                s = jnp.where(valid, s, NEG_INF)
            e = jnp.exp2(s - jnp.max(s, axis=1, keepdims=True))
            out.append((e.astype(BF16), jnp.sum(e, axis=1, keepdims=True)))
        return out

    def pv(c, e_pairs):
        for p in range(N_PAIRS):
            sl = slice(p * LANES, (p + 1) * LANES)
            e, den = e_pairs[p]
            vb = band(vwin, c, p)
            o2 = jnp.dot(e, vb, preferred_element_type=F32) / den
            ybuf[c * CHUNK:(c + 1) * CHUNK, sl] = jnp.where(low_c, o2[0:CHUNK], o2[CHUNK:2 * CHUNK]).astype(BF16)

    z_a = sec(2 * W, W)
    z_b = sec(3 * W, W)
    glu = z_a * jax.nn.sigmoid(z_b)
    for s in range(n_seq):
        cbuf[s, CONV_OFF:CONV_OFF + Ts, :] = glu[seq_rows[s]]
    k = sec(5 * W, W)
    add_conv_taps(0, 1)
    va = sec(6 * W, W)
    add_conv_taps(2, 3)
    for p in range(N_PAIRS):
        sl = slice(p * LANES, (p + 1) * LANES)
        k_n = pair_unit(k[:, sl]) * gk_ref[:, sl]
        for s in range(n_seq):
            kst_ref[s, :, sl] = k_n[seq_rows[s]]
            kwin[s, hist:hist + Ts, sl] = k_n[seq_rows[s]].astype(BF16)
    q = sec(4 * W, W)
    add_conv_taps(4, 5)
    for s in range(n_seq):
        vst_ref[s] = va[seq_rows[s]]
        vwin[s, hist:hist + Ts, :] = va[seq_rows[s]].astype(BF16)
    z_u = sec(0, W)
    add_conv_taps(6, 7)
    q_lo, q_hi = [], []
    for p in range(N_PAIRS):
        sl = slice(p * LANES, (p + 1) * LANES)
        q_unit = pair_unit(q[:, sl])
        q_lo.append((q_unit * gqlo_ref[:, sl]).astype(BF16))
        q_hi.append((q_unit * gqhi_ref[:, sl]).astype(BF16))
    z_v = sec(W, W)
    conv = conv_acc[0] if n_seq == 1 else jnp.concatenate(conv_acc, axis=0)
    y_b = jax.nn.silu(_rms(conv + cb_ref[...], gconv_ref[...])).astype(BF16)
    for s in range(n_seq):
        cst_ref[s] = cbuf[s, base + Ts:base + Ts + CONV_HIST, :]
        cbuf[s, 0:CONV_OFF, :] = cbuf[s, Ts:Ts + CONV_OFF, :]
    u = _gelu(z_u)
    g_pre0 = gate_pre(0)
    v_n = _rms(_gelu(z_v), ggmlp_ref[...])
    if emit_vn:
        for s in range(n_seq):
            vn_ref[s] = v_n[seq_rows[s]]
    v_nb = v_n.astype(BF16)
    g_pre1 = gate_pre(1)
    gate0 = jax.nn.sigmoid(g_pre0)
    g_pre2 = gate_pre(2)
    gate1 = jax.nn.sigmoid(g_pre1)

    s_pairs = scores(0)
    gate2 = jax.nn.sigmoid(g_pre2)
    r_i = lax.broadcasted_iota(jnp.int32, (GMLP_GROUPS, gc, gc), 1)
    c_i = lax.broadcasted_iota(jnp.int32, (GMLP_GROUPS, gc, gc), 2)
    wm = jnp.where(r_i >= c_i, ws_ref[:, 0:gc, 0:gc], 0.0).astype(BF16)
    row_blocks = []
    for n in range(T // gc):
        cols = [jnp.dot(wm[g], v_nb[n * gc:(n + 1) * gc, g * GMLP_GW:(g + 1) * GMLP_GW],
                        preferred_element_type=F32) for g in range(GMLP_GROUPS)]
        row_blocks.append(jnp.concatenate(cols, axis=1) + gbias_ref[0:gc, :])
    mixed = row_blocks[0] if len(row_blocks) == 1 else jnp.concatenate(row_blocks, axis=0)
    y_a = (u * mixed).astype(BF16)
    merged = None
    for c in range(n_chunks):
        s_next = scores(c + 1) if c + 1 < n_chunks else None
        e_pairs = softmax(c, s_pairs)
        if c == 0:
            merged = gate0 * jnp.dot(y_a, wbr_ref[0], preferred_element_type=F32)
        if c == min(1, n_chunks - 1):
            merged = merged + gate1 * jnp.dot(y_b, wbr_ref[1], preferred_element_type=F32)
        pv(c, e_pairs)
        s_pairs = s_next

    merged = merged + gate2 * jnp.dot(ybuf[...], wbr_ref[2], preferred_element_type=F32)
    x_out = rows_of(x_ref) + jnp.dot(merged.astype(BF16), wout_ref[...], preferred_element_type=F32)
    for s in range(n_seq):
        xo_ref[s] = x_out[seq_rows[s]]

    if carry:
        for s in range(n_seq):
            kwin[s, KPAD:hist, :] = kwin[s, KPAD + Ts:hist + Ts, :]
            vwin[s, KPAD:hist, :] = vwin[s, KPAD + Ts:hist + Ts, :]


def _layer_spec(a, layer):
    zeros = (0,) * (a.ndim - 1)
    return pl.BlockSpec((None,) + a.shape[1:], lambda *g: (layer,) + zeros, pipeline_mode=pl.Buffered(1))


def _mixer(x, cache, p, layer, *, Ts, n_seq, emit_vn):
    B, L, _ = x.shape
    W = BRANCH_W
    nT = L // Ts
    keep = min(WINDOW, L)
    nkeep = keep // Ts
    gc = min(L, GMLP_CHUNK)
    has_cache = cache is not None
    assert L % Ts == 0 and Ts % CHUNK == 0 and Ts % gc == 0 and keep % Ts == 0 and Ts <= WINDOW and B % n_seq == 0

    tile = lambda b, t: (b, t, 0)
    per_b = lambda b, t: (b, 0, 0)
    state = lambda b, t: (b, jnp.maximum(t - (nT - nkeep), 0), 0)

    args = [x]
    in_specs = [pl.BlockSpec((n_seq, Ts, D_MODEL), tile)]
    if has_cache:
        ck, cv, cc = cache
        args += [ck.reshape(B, WINDOW, W), cv.reshape(B, WINDOW, W), cc]
        in_specs += [pl.BlockSpec((n_seq, WINDOW, W), per_b), pl.BlockSpec((n_seq, WINDOW, W), per_b),
                     pl.BlockSpec((n_seq, CONV_HIST, W), per_b)]
    params = [p[name] for name in ("g_mix", "w_in", "b_gate", "g_gmlp", "ws", "gbias", "conv_dw", "conv_b",
                                   "g_conv", "gq_lo", "gq_hi", "g_k", "relb", "w_branch", "w_out")]
    args += params
    in_specs += [_layer_spec(a, layer) for a in params]

    out_shape = [jax.ShapeDtypeStruct((B, L, D_MODEL), F32),
                 jax.ShapeDtypeStruct((B, keep, W), F32),
                 jax.ShapeDtypeStruct((B, keep, W), F32),
                 jax.ShapeDtypeStruct((B, CONV_HIST, W), F32)]
    out_specs = [pl.BlockSpec((n_seq, Ts, D_MODEL), tile),
                 pl.BlockSpec((n_seq, Ts, W), state),
                 pl.BlockSpec((n_seq, Ts, W), state),
                 pl.BlockSpec((n_seq, CONV_HIST, W), per_b)]
    if emit_vn:
        out_shape.append(jax.ShapeDtypeStruct((B, L, W), F32))
        out_specs.append(pl.BlockSpec((n_seq, Ts, W), tile))

    win_rows = KPAD + WINDOW + Ts
    return pl.pallas_call(
        functools.partial(_mixer_kernel, n_seq=n_seq, Ts=Ts, has_cache=has_cache, emit_vn=emit_vn, gc=gc,
                          carry=nT > 1),
        grid=(B // n_seq, nT),
        in_specs=in_specs,
        out_specs=out_specs,
        out_shape=out_shape,
        scratch_shapes=[pltpu.VMEM((n_seq, win_rows, W), BF16), pltpu.VMEM((n_seq, win_rows, W), BF16),
                        pltpu.VMEM((n_seq, CONV_OFF + Ts, W), F32), pltpu.VMEM((n_seq * Ts, W), BF16)],
        compiler_params=pltpu.CompilerParams(dimension_semantics=("arbitrary", "arbitrary"),
                                             vmem_limit_bytes=VMEM_LIMIT),
        name="mixer_cache" if has_cache else "mixer_prompt",
    )(*args)


def _ffn_kernel(x_ref, g_ref, wgu_ref, wd_ref, o_ref):
    x = x_ref[...]
    h2 = _rms(x, g_ref[...]).astype(BF16)
    acc = x
    for j in range(FFN_HIDDEN // FFN_HC):
        lo = j * FFN_HC
        g = jnp.dot(h2, wgu_ref[:, lo:lo + FFN_HC], preferred_element_type=F32)
        u = jnp.dot(h2, wgu_ref[:, FFN_HIDDEN + lo:FFN_HIDDEN + lo + FFN_HC], preferred_element_type=F32)
        a = (jax.nn.silu(g) * u).astype(BF16)
        acc = acc + jnp.dot(a, wd_ref[lo:lo + FFN_HC, :], preferred_element_type=F32)
    o_ref[...] = acc


def _ffn(x, p, layer, *, T):
    B, L, _ = x.shape
    rows = B * L
    assert rows % T == 0 and FFN_HIDDEN % FFN_HC == 0
    params = [p["g_ffn"], p["w_gate_up"], p["w_down"]]
    out = pl.pallas_call(
        _ffn_kernel,
        grid=(rows // T,),
        in_specs=[pl.BlockSpec((T, D_MODEL), lambda i: (i, 0))] + [_layer_spec(a, layer) for a in params],
        out_specs=pl.BlockSpec((T, D_MODEL), lambda i: (i, 0)),
        out_shape=jax.ShapeDtypeStruct((rows, D_MODEL), F32),
        compiler_params=pltpu.CompilerParams(dimension_semantics=("arbitrary",),
                                             vmem_limit_bytes=VMEM_LIMIT),
        name="ffn",
    )(x.reshape(rows, D_MODEL), *params)
    return out.reshape(B, L, D_MODEL)


def _rel_bias_table(rel_bias):
    n_heads = rel_bias.shape[0]
    n_far = BAND_PAD - MAX_REL
    period = BAND_PAD + CHUNK
    far = jnp.broadcast_to(rel_bias[:, 2 * MAX_REL:2 * MAX_REL + 1], (n_heads, n_far))
    near = jnp.flip(rel_bias[:, 2 * MAX_REL - (period - n_far):2 * MAX_REL], axis=1)
    g = jnp.concatenate([far, near], axis=1) * LOG2E
    skew = jnp.tile(g, (1, CHUNK))[:, :CHUNK * (period - 1)].reshape(n_heads, CHUNK, period - 1)
    tbl = skew[:, :, CHUNK - 1:CHUNK - 1 + BAND_PAD]
    tbl = jnp.where(jnp.arange(BAND_PAD)[None, None, :] >= KPAD, tbl, NEG_INF)
    return tbl.reshape(n_heads // 2, 2 * CHUNK, BAND_PAD).astype(F32)


def kernel(x_prompt, x_sample, cache_attn_k, cache_attn_v, cache_conv, norm_mix_g, w_in, b_gate, gmlp_norm_g, gmlp_ws, gmlp_bs, conv_dw, conv_b, conv_norm_g, q_norm_g, k_norm_g, rel_bias, w_branch, w_out, norm_ffn_g, w_gate_up, w_down):
    depth = w_in.shape[0]
    xp, xs = x_prompt, x_sample
    Bp, Bs, Ls = xp.shape[0], xs.shape[0], xs.shape[1]
    rows = lambda a: a.reshape(depth, 1, -1)
    lane_low = (jnp.arange(BRANCH_W) % LANES) < HEAD_DIM
    gq = jnp.tile(q_norm_g, (1, N_HEADS)) * (ATTN_SCALE * LOG2E)
    p = {
        "w_in": w_in.astype(BF16), "w_branch": w_branch.astype(BF16), "w_out": w_out.astype(BF16),
        "w_gate_up": w_gate_up.astype(BF16), "w_down": w_down.astype(BF16),
        "g_mix": rows(norm_mix_g), "b_gate": rows(b_gate), "g_gmlp": rows(gmlp_norm_g), "ws": gmlp_ws,
        "gbias": jnp.repeat(jnp.swapaxes(gmlp_bs, 1, 2), GMLP_GW, axis=2),
        "conv_dw": conv_dw, "conv_b": rows(conv_b), "g_conv": rows(conv_norm_g),
        "gq_lo": rows(jnp.where(lane_low, gq, 0.0)), "gq_hi": rows(jnp.where(lane_low, 0.0, gq)),
        "g_k": rows(jnp.tile(k_norm_g, (1, N_HEADS))),
        "relb": _rel_bias_table(rel_bias.reshape(depth * N_HEADS, -1)).reshape(depth, N_PAIRS, 2 * CHUNK, BAND_PAD),
        "g_ffn": rows(norm_ffn_g),
    }
    kp_l, vp_l, cp_l, ks_l, vs_l, cs_l, gs_l = [], [], [], [], [], [], []
    for l in range(depth):
        xp, kp, vp, cp = _mixer(xp, None, p, l, Ts=MIXER_TILE, n_seq=1, emit_vn=False)
        xp = _ffn(xp, p, l, T=FFN_TILE)
        xs, ks, vs, cs, gs = _mixer(xs, (cache_attn_k[l], cache_attn_v[l], cache_conv[l]), p, l, Ts=Ls,
                                    n_seq=SAMPLE_SEQS, emit_vn=True)
        xs = _ffn(xs, p, l, T=Bs * Ls)
        kp_l.append(kp.reshape(Bp, -1, N_HEADS, HEAD_DIM))
        vp_l.append(vp.reshape(Bp, -1, N_HEADS, HEAD_DIM))
        cp_l.append(cp)
        ks_l.append(ks.reshape(Bs, Ls, N_HEADS, HEAD_DIM))
        vs_l.append(vs.reshape(Bs, Ls, N_HEADS, HEAD_DIM))
        cs_l.append(cs)
        gs_l.append(gs)
    return (xp, xs, jnp.stack(kp_l), jnp.stack(vp_l), jnp.stack(cp_l),
            jnp.stack(ks_l), jnp.stack(vs_l), jnp.stack(cs_l), jnp.stack(gs_l))
```

```python
import functools
import math

import jax
import jax.numpy as jnp
from jax import lax
from jax.experimental import pallas as pl
from jax.experimental.pallas import tpu as pltpu

D_MODEL = 1024
BRANCH_W = 512
N_BRANCH = 3
GMLP_CHUNK = 128
GMLP_GROUPS = 4
GMLP_GW = BRANCH_W // GMLP_GROUPS
CONV_W = 31
N_HEADS = 8
HEAD_DIM = 64
CHUNK = 64
WINDOW = 512
BAND = WINDOW + CHUNK
MAX_REL = 128
FFN_HIDDEN = 2816
IN_COLS = 7 * BRANCH_W + N_BRANCH * D_MODEL
ATTN_SCALE = HEAD_DIM ** -0.5
NEG_INF = -1e30
EPS = 1e-6

LANES = 128
SUBLANES = 8
BF16_ROWS = 16
N_PAIRS = N_HEADS // 2
KPAD = CHUNK
BAND_PAD = BAND + KPAD
CONV_HIST = CONV_W - 1
CONV_OFF = 32
VMEM_LIMIT = 56 * 1024 * 1024
MIXER_TILE = 512
SAMPLE_SEQS = 2
FFN_TILE = 1024
FFN_HC = 256

F32 = jnp.float32
BF16 = jnp.bfloat16
NT_DIMS = (((1,), (1,)), ((), ()))

GELU_A = 2.0 * math.sqrt(2.0 / math.pi)
GELU_B = GELU_A * 0.044715
LOG2E = math.log2(math.e)


def _rms(x, g):
    return x * lax.rsqrt(jnp.mean(x * x, axis=-1, keepdims=True) + EPS) * g


def _gelu(x):
    return x * jax.nn.sigmoid(x * (GELU_A + GELU_B * (x * x)))


def _mixer_kernel(*refs, n_seq, Ts, has_cache, emit_vn, gc, carry):
    it = iter(refs)
    x_ref = next(it)
    if has_cache:
        ck_ref, cv_ref, cc_ref = next(it), next(it), next(it)
    gmix_ref, win_ref, bgate_ref = next(it), next(it), next(it)
    ggmlp_ref, ws_ref, gbias_ref = next(it), next(it), next(it)
    cdw_ref, cb_ref, gconv_ref = next(it), next(it), next(it)
    gqlo_ref, gqhi_ref, gk_ref, relb_ref = next(it), next(it), next(it), next(it)
    wbr_ref, wout_ref = next(it), next(it)
    xo_ref, kst_ref, vst_ref, cst_ref = next(it), next(it), next(it), next(it)
    if emit_vn:
        vn_ref = next(it)
    kwin, vwin, cbuf, ybuf = next(it), next(it), next(it), next(it)

    t = pl.program_id(1)
    W = BRANCH_W
    T = n_seq * Ts
    hist = KPAD + WINDOW
    seq_chunks = Ts // CHUNK
    n_chunks = n_seq * seq_chunks
    base = CONV_OFF - CONV_HIST
    seq_rows = [slice(s * Ts, (s + 1) * Ts) for s in range(n_seq)]

    def rows_of(ref):
        return ref[0] if n_seq == 1 else jnp.concatenate([ref[s] for s in range(n_seq)], axis=0)

    @pl.when(t == 0)
    def _init():
        for s in range(n_seq):
            if has_cache:
                kwin[s, 0:KPAD, :] = jnp.zeros((KPAD, W), BF16)
                vwin[s, 0:KPAD, :] = jnp.zeros((KPAD, W), BF16)
                kwin[s, KPAD:hist, :] = ck_ref[s].astype(BF16)
                vwin[s, KPAD:hist, :] = cv_ref[s].astype(BF16)
                cbuf[s, 0:CONV_OFF, :] = jnp.zeros((CONV_OFF, W), F32)
                cbuf[s, base:CONV_OFF, :] = cc_ref[s]
            else:
                kwin[s, 0:hist, :] = jnp.zeros((hist, W), BF16)
                vwin[s, 0:hist, :] = jnp.zeros((hist, W), BF16)
                cbuf[s, 0:CONV_OFF, :] = jnp.zeros((CONV_OFF, W), F32)

    h = _rms(rows_of(x_ref), gmix_ref[...]).astype(BF16)

    def sec(lo, width):
        return jnp.dot(h, win_ref[:, lo:lo + width], preferred_element_type=F32)

    def gate_pre(n):
        return sec(7 * W + n * D_MODEL, D_MODEL) + bgate_ref[:, n * D_MODEL:(n + 1) * D_MODEL]

    conv_acc = [None] * n_seq

    def add_conv_taps(*residues):
        for s in range(n_seq):
            for r in residues:
                part = None
                for off in range(base, base + CONV_W):
                    if off % SUBLANES == r:
                        term = cbuf[s, off:off + Ts, :] * cdw_ref[off - base:off - base + 1, :]
                        part = term if part is None else part + term
                conv_acc[s] = part if conv_acc[s] is None else conv_acc[s] + part

    low = lax.broadcasted_iota(jnp.int32, (T, LANES), 1) < HEAD_DIM

    def pair_unit(zp):
        sq = zp * zp
        s0 = jnp.sum(jnp.where(low, sq, 0.0), axis=1, keepdims=True)
        s1 = jnp.sum(jnp.where(low, 0.0, sq), axis=1, keepdims=True)
        r = jnp.where(low, lax.rsqrt(s0 * (1.0 / HEAD_DIM) + EPS), lax.rsqrt(s1 * (1.0 / HEAD_DIM) + EPS))
        return zp * r

    def band(win, c, p):
        s, lo = c // seq_chunks, (c % seq_chunks) * CHUNK
        return win[s, lo:lo + BAND_PAD, p * LANES:(p + 1) * LANES]

    def scores(c):
        rows = slice(c * CHUNK, (c + 1) * CHUNK)
        out = []
        for p in range(N_PAIRS):
            q2 = jnp.concatenate([q_lo[p][rows], q_hi[p][rows]], axis=0)
            kb = band(kwin, c, p)
            out.append(lax.dot_general(q2, kb, NT_DIMS, preferred_element_type=F32))
        return out

    col = lax.broadcasted_iota(jnp.int32, (2 * CHUNK, BAND_PAD), 1)
    low_c = lax.broadcasted_iota(jnp.int32, (CHUNK, LANES), 1) < HEAD_DIM

    def softmax(c, s_pairs):
        if not has_cache:
            valid = col >= (hist - t * Ts - (c % seq_chunks) * CHUNK)
        out = []
        for p in range(N_PAIRS):
            s = s_pairs[p] + relb_ref[p]
            if not has_cache:
                s = jnp.where(valid, s, NEG_INF)
            e = jnp.exp2(s - jnp.max(s, axis=1, keepdims=True))
            out.append((e.astype(BF16), jnp.sum(e, axis=1, keepdims=True)))
        return out

    def pv(c, e_pairs):
        for p in range(N_PAIRS):
            sl = slice(p * LANES, (p + 1) * LANES)
            e, den = e_pairs[p]
            vb = band(vwin, c, p)
            o2 = jnp.dot(e, vb, preferred_element_type=F32) / den
            ybuf[c * CHUNK:(c + 1) * CHUNK, sl] = jnp.where(low_c, o2[0:CHUNK], o2[CHUNK:2 * CHUNK]).astype(BF16)

    z_a = sec(2 * W, W)
    z_b = sec(3 * W, W)
    glu = z_a * jax.nn.sigmoid(z_b)
    for s in range(n_seq):
        cbuf[s, CONV_OFF:CONV_OFF + Ts, :] = glu[seq_rows[s]]
    k = sec(5 * W, W)
    add_conv_taps(0, 1)
    va = sec(6 * W, W)
    add_conv_taps(2, 3)
    for p in range(N_PAIRS):
        sl = slice(p * LANES, (p + 1) * LANES)
        k_n = pair_unit(k[:, sl]) * gk_ref[:, sl]
        for s in range(n_seq):
            kst_ref[s, :, sl] = k_n[seq_rows[s]]
            kwin[s, hist:hist + Ts, sl] = k_n[seq_rows[s]].astype(BF16)
    q = sec(4 * W, W)
    add_conv_taps(4, 5)
    for s in range(n_seq):
        vst_ref[s] = va[seq_rows[s]]
        vwin[s, hist:hist + Ts, :] = va[seq_rows[s]].astype(BF16)
    z_u = sec(0, W)
    add_conv_taps(6, 7)
    q_lo, q_hi = [], []
    for p in range(N_PAIRS):
        sl = slice(p * LANES, (p + 1) * LANES)
        q_unit = pair_unit(q[:, sl])
        q_lo.append((q_unit * gqlo_ref[:, sl]).astype(BF16))
        q_hi.append((q_unit * gqhi_ref[:, sl]).astype(BF16))
    z_v = sec(W, W)
    conv = conv_acc[0] if n_seq == 1 else jnp.concatenate(conv_acc, axis=0)
    y_b = jax.nn.silu(_rms(conv + cb_ref[...], gconv_ref[...])).astype(BF16)
    for s in range(n_seq):
        cst_ref[s] = cbuf[s, base + Ts:base + Ts + CONV_HIST, :]
        cbuf[s, 0:CONV_OFF, :] = cbuf[s, Ts:Ts + CONV_OFF, :]
    u = _gelu(z_u)
    g_pre0 = gate_pre(0)
    v_n = _rms(_gelu(z_v), ggmlp_ref[...])
    if emit_vn:
        for s in range(n_seq):
            vn_ref[s] = v_n[seq_rows[s]]
    v_nb = v_n.astype(BF16)
    g_pre1 = gate_pre(1)
    gate0 = jax.nn.sigmoid(g_pre0)
    g_pre2 = gate_pre(2)
    gate1 = jax.nn.sigmoid(g_pre1)

    s_pairs = scores(0)
    gate2 = jax.nn.sigmoid(g_pre2)
    r_i = lax.broadcasted_iota(jnp.int32, (GMLP_GROUPS, gc, gc), 1)
    c_i = lax.broadcasted_iota(jnp.int32, (GMLP_GROUPS, gc, gc), 2)
    wm = jnp.where(r_i >= c_i, ws_ref[:, 0:gc, 0:gc], 0.0).astype(BF16)
    row_blocks = []
    for n in range(T // gc):
        cols = [jnp.dot(wm[g], v_nb[n * gc:(n + 1) * gc, g * GMLP_GW:(g + 1) * GMLP_GW],
                        preferred_element_type=F32) for g in range(GMLP_GROUPS)]
        row_blocks.append(jnp.concatenate(cols, axis=1) + gbias_ref[0:gc, :])
    mixed = row_blocks[0] if len(row_blocks) == 1 else jnp.concatenate(row_blocks, axis=0)
    y_a = (u * mixed).astype(BF16)
    merged = None
    for c in range(n_chunks):
        s_next = scores(c + 1) if c + 1 < n_chunks else None
        e_pairs = softmax(c, s_pairs)
        if c == 0:
            merged = gate0 * jnp.dot(y_a, wbr_ref[0], preferred_element_type=F32)
        if c == min(1, n_chunks - 1):
            merged = merged + gate1 * jnp.dot(y_b, wbr_ref[1], preferred_element_type=F32)
        pv(c, e_pairs)
        s_pairs = s_next

    merged = merged + gate2 * jnp.dot(ybuf[...], wbr_ref[2], preferred_element_type=F32)
    x_out = rows_of(x_ref) + jnp.dot(merged.astype(BF16), wout_ref[...], preferred_element_type=F32)
    for s in range(n_seq):
        xo_ref[s] = x_out[seq_rows[s]]

    if carry:
        for s in range(n_seq):
            kwin[s, KPAD:hist, :] = kwin[s, KPAD + Ts:hist + Ts, :]
            vwin[s, KPAD:hist, :] = vwin[s, KPAD + Ts:hist + Ts, :]


def _layer_spec(a, layer):
    zeros = (0,) * (a.ndim - 1)
    return pl.BlockSpec((None,) + a.shape[1:], lambda *g: (layer,) + zeros, pipeline_mode=pl.Buffered(1))


def _mixer(x, cache, p, wl, layer, *, Ts, n_seq, emit_vn):
    B, L, _ = x.shape
    W = BRANCH_W
    nT = L // Ts
    keep = min(WINDOW, L)
    nkeep = keep // Ts
    gc = min(L, GMLP_CHUNK)
    has_cache = cache is not None
    assert L % Ts == 0 and Ts % CHUNK == 0 and Ts % gc == 0 and keep % Ts == 0 and Ts <= WINDOW and B % n_seq == 0

    tile = lambda b, t: (b, t, 0)
    per_b = lambda b, t: (b, 0, 0)
    state = lambda b, t: (b, jnp.maximum(t - (nT - nkeep), 0), 0)

    args = [x]
    in_specs = [pl.BlockSpec((n_seq, Ts, D_MODEL), tile)]
    if has_cache:
        ck, cv, cc = cache
        args += [ck.reshape(B, WINDOW, W), cv.reshape(B, WINDOW, W), cc]
        in_specs += [pl.BlockSpec((n_seq, WINDOW, W), per_b), pl.BlockSpec((n_seq, WINDOW, W), per_b),
                     pl.BlockSpec((n_seq, CONV_HIST, W), per_b)]
    for name in ("g_mix", "w_in", "b_gate", "g_gmlp", "ws", "gbias", "conv_dw", "conv_b",
                 "g_conv", "gq_lo", "gq_hi", "g_k", "relb", "w_branch", "w_out"):
        a, index = (wl[name], 0) if name in wl else (p[name], layer)
        args.append(a)
        in_specs.append(_layer_spec(a, index))

    out_shape = [jax.ShapeDtypeStruct((B, L, D_MODEL), F32),
                 jax.ShapeDtypeStruct((B, keep, W), F32),
                 jax.ShapeDtypeStruct((B, keep, W), F32),
                 jax.ShapeDtypeStruct((B, CONV_HIST, W), F32)]
    out_specs = [pl.BlockSpec((n_seq, Ts, D_MODEL), tile),
                 pl.BlockSpec((n_seq, Ts, W), state),
                 pl.BlockSpec((n_seq, Ts, W), state),
                 pl.BlockSpec((n_seq, CONV_HIST, W), per_b)]
    if emit_vn:
        out_shape.append(jax.ShapeDtypeStruct((B, L, W), F32))
        out_specs.append(pl.BlockSpec((n_seq, Ts, W), tile))

    win_rows = KPAD + WINDOW + Ts
    return pl.pallas_call(
        functools.partial(_mixer_kernel, n_seq=n_seq, Ts=Ts, has_cache=has_cache, emit_vn=emit_vn, gc=gc,
                          carry=nT > 1),
        grid=(B // n_seq, nT),
        in_specs=in_specs,
        out_specs=out_specs,
        out_shape=out_shape,
        scratch_shapes=[pltpu.VMEM((n_seq, win_rows, W), BF16), pltpu.VMEM((n_seq, win_rows, W), BF16),
                        pltpu.VMEM((n_seq, CONV_OFF + Ts, W), F32), pltpu.VMEM((n_seq * Ts, W), BF16)],
        compiler_params=pltpu.CompilerParams(dimension_semantics=("arbitrary", "arbitrary"),
                                             vmem_limit_bytes=VMEM_LIMIT),
        name="mixer_cache" if has_cache else "mixer_prompt",
    )(*args)


def _ffn_kernel(x_ref, g_ref, wgu_ref, wd_ref, *rest):
    n_cast = len(rest) // 2
    o_ref = rest[n_cast]
    for src, dst in zip(rest[:n_cast], rest[n_cast + 1:]):
        dst[...] = src[...].astype(BF16)
    x = x_ref[...]
    h2 = _rms(x, g_ref[...]).astype(BF16)
    acc = x
    for j in range(FFN_HIDDEN // FFN_HC):
        lo = j * FFN_HC
        g = jnp.dot(h2, wgu_ref[:, lo:lo + FFN_HC], preferred_element_type=F32)
        u = jnp.dot(h2, wgu_ref[:, FFN_HIDDEN + lo:FFN_HIDDEN + lo + FFN_HC], preferred_element_type=F32)
        a = (jax.nn.silu(g) * u).astype(BF16)
        acc = acc + jnp.dot(a, wd_ref[lo:lo + FFN_HC, :], preferred_element_type=F32)
    o_ref[...] = acc


def _ffn(x, p, wl, layer, *, T, cast_next=None):
    B, L, _ = x.shape
    rows = B * L
    n_steps = rows // T
    assert rows % T == 0 and FFN_HIDDEN % FFN_HC == 0
    tile = pl.BlockSpec((T, D_MODEL), lambda i: (i, 0))
    args = [x.reshape(rows, D_MODEL), p["g_ffn"], wl["w_gate_up"], wl["w_down"]]
    in_specs = [tile, _layer_spec(p["g_ffn"], layer), _layer_spec(wl["w_gate_up"], 0), _layer_spec(wl["w_down"], 0)]
    out_shape = [jax.ShapeDtypeStruct((rows, D_MODEL), F32)]
    out_specs = [tile]
    names = []
    if cast_next is not None:
        stacked, nxt = cast_next
        for name, w in stacked.items():
            depth, r, c = w.shape[0], math.prod(w.shape[1:-1]), w.shape[-1]
            n_c = max(d for d in range(1, n_steps + 1) if n_steps % d == 0 and r % (d * BF16_ROWS) == 0)
            every = n_steps // n_c
            names.append(name)
            args.append(w.reshape(depth, n_c, r // n_c, c))
            in_specs.append(pl.BlockSpec((None, None, r // n_c, c), lambda i, every=every: (nxt, i // every, 0, 0)))
            out_shape.append(jax.ShapeDtypeStruct((n_c, r // n_c, c), BF16))
            out_specs.append(pl.BlockSpec((None, r // n_c, c), lambda i, every=every: (i // every, 0, 0)))
    outs = pl.pallas_call(
        _ffn_kernel,
        grid=(n_steps,),
        in_specs=in_specs,
        out_specs=out_specs,
        out_shape=out_shape,
        compiler_params=pltpu.CompilerParams(dimension_semantics=("arbitrary",),
                                             vmem_limit_bytes=VMEM_LIMIT),
        name="ffn",
    )(*args)
    wl_next = {name: o.reshape((1,) + cast_next[0][name].shape[1:]) for name, o in zip(names, outs[1:])}
    return outs[0].reshape(B, L, D_MODEL), wl_next


def _rel_bias_table(rel_bias):
    n_heads = rel_bias.shape[0]
    n_far = BAND_PAD - MAX_REL
    period = BAND_PAD + CHUNK
    far = jnp.broadcast_to(rel_bias[:, 2 * MAX_REL:2 * MAX_REL + 1], (n_heads, n_far))
    near = jnp.flip(rel_bias[:, 2 * MAX_REL - (period - n_far):2 * MAX_REL], axis=1)
    g = jnp.concatenate([far, near], axis=1) * LOG2E
    skew = jnp.tile(g, (1, CHUNK))[:, :CHUNK * (period - 1)].reshape(n_heads, CHUNK, period - 1)
    tbl = skew[:, :, CHUNK - 1:CHUNK - 1 + BAND_PAD]
    tbl = jnp.where(jnp.arange(BAND_PAD)[None, None, :] >= KPAD, tbl, NEG_INF)
    return tbl.reshape(n_heads // 2, 2 * CHUNK, BAND_PAD).astype(F32)


def kernel(x_prompt, x_sample, cache_attn_k, cache_attn_v, cache_conv, norm_mix_g, w_in, b_gate, gmlp_norm_g, gmlp_ws, gmlp_bs, conv_dw, conv_b, conv_norm_g, q_norm_g, k_norm_g, rel_bias, w_branch, w_out, norm_ffn_g, w_gate_up, w_down):
    depth = w_in.shape[0]
    xp, xs = x_prompt, x_sample
    Bp, Bs, Ls = xp.shape[0], xs.shape[0], xs.shape[1]
    rows = lambda a: a.reshape(depth, 1, -1)
    lane_low = (jnp.arange(BRANCH_W) % LANES) < HEAD_DIM
    gq = jnp.tile(q_norm_g, (1, N_HEADS)) * (ATTN_SCALE * LOG2E)
    weights = {"w_in": w_in, "w_branch": w_branch, "w_out": w_out, "w_gate_up": w_gate_up, "w_down": w_down}
    wl = {name: w[0:1].astype(BF16) for name, w in weights.items()}
    p = {
        "g_mix": rows(norm_mix_g), "b_gate": rows(b_gate), "g_gmlp": rows(gmlp_norm_g), "ws": gmlp_ws,
        "gbias": jnp.repeat(jnp.swapaxes(gmlp_bs, 1, 2), GMLP_GW, axis=2),
        "conv_dw": conv_dw, "conv_b": rows(conv_b), "g_conv": rows(conv_norm_g),
        "gq_lo": rows(jnp.where(lane_low, gq, 0.0)), "gq_hi": rows(jnp.where(lane_low, 0.0, gq)),
        "g_k": rows(jnp.tile(k_norm_g, (1, N_HEADS))),
        "relb": _rel_bias_table(rel_bias.reshape(depth * N_HEADS, -1)).reshape(depth, N_PAIRS, 2 * CHUNK, BAND_PAD),
        "g_ffn": rows(norm_ffn_g),
    }
    kp_l, vp_l, cp_l, ks_l, vs_l, cs_l, gs_l = [], [], [], [], [], [], []
    for l in range(depth):
        xp, kp, vp, cp = _mixer(xp, None, p, wl, l, Ts=MIXER_TILE, n_seq=1, emit_vn=False)
        xp, wl_next = _ffn(xp, p, wl, l, T=FFN_TILE, cast_next=(weights, l + 1) if l + 1 < depth else None)
        xs, ks, vs, cs, gs = _mixer(xs, (cache_attn_k[l], cache_attn_v[l], cache_conv[l]), p, wl, l, Ts=Ls,
                                    n_seq=SAMPLE_SEQS, emit_vn=True)
        xs, _ = _ffn(xs, p, wl, l, T=Bs * Ls)
        wl = wl_next
        kp_l.append(kp.reshape(Bp, -1, N_HEADS, HEAD_DIM))
        vp_l.append(vp.reshape(Bp, -1, N_HEADS, HEAD_DIM))
        cp_l.append(cp)
        ks_l.append(ks.reshape(Bs, Ls, N_HEADS, HEAD_DIM))
        vs_l.append(vs.reshape(Bs, Ls, N_HEADS, HEAD_DIM))
        cs_l.append(cs)
        gs_l.append(gs)
    return (xp, xs, jnp.stack(kp_l), jnp.stack(vp_l), jnp.stack(cp_l),
            jnp.stack(ks_l), jnp.stack(vs_l), jnp.stack(cs_l), jnp.stack(gs_l))
```

```python
import functools
import math

import jax
import jax.numpy as jnp
from jax import lax
from jax.experimental import pallas as pl
from jax.experimental.pallas import tpu as pltpu

D_MODEL = 1024
BRANCH_W = 512
N_BRANCH = 3
GMLP_CHUNK = 128
GMLP_GROUPS = 4
GMLP_GW = BRANCH_W // GMLP_GROUPS
CONV_W = 31
N_HEADS = 8
HEAD_DIM = 64
CHUNK = 64
WINDOW = 512
BAND = WINDOW + CHUNK
MAX_REL = 128
FFN_HIDDEN = 2816
IN_COLS = 7 * BRANCH_W + N_BRANCH * D_MODEL
ATTN_SCALE = HEAD_DIM ** -0.5
NEG_INF = -1e30
EPS = 1e-6

LANES = 128
SUBLANES = 8
BF16_ROWS = 16
N_PAIRS = N_HEADS // 2
KPAD = CHUNK
BAND_PAD = BAND + KPAD
CONV_HIST = CONV_W - 1
CONV_OFF = 32
VMEM_LIMIT = 56 * 1024 * 1024
MIXER_TILE = 512
SAMPLE_SEQS = 2
FFN_TILE = 1024
FFN_HC = 256

F32 = jnp.float32
BF16 = jnp.bfloat16
NT_DIMS = (((1,), (1,)), ((), ()))

GELU_A = 2.0 * math.sqrt(2.0 / math.pi)
GELU_B = GELU_A * 0.044715
LOG2E = math.log2(math.e)


def _rms(x, g):
    return x * lax.rsqrt(jnp.mean(x * x, axis=-1, keepdims=True) + EPS) * g


def _gelu(x):
    return x * jax.nn.sigmoid(x * (GELU_A + GELU_B * (x * x)))


def _mixer_kernel(*refs, n_seq, Ts, has_cache, emit_vn, gc, carry):
    it = iter(refs)
    x_ref = next(it)
    if has_cache:
        ck_ref, cv_ref, cc_ref = next(it), next(it), next(it)
    gmix_ref, win_ref, bgate_ref = next(it), next(it), next(it)
    ggmlp_ref, ws_ref, gbias_ref = next(it), next(it), next(it)
    cdw_ref, cb_ref, gconv_ref = next(it), next(it), next(it)
    gqlo_ref, gqhi_ref, gk_ref, relb_ref = next(it), next(it), next(it), next(it)
    wbr_ref, wout_ref = next(it), next(it)
    xo_ref, kst_ref, vst_ref, cst_ref = next(it), next(it), next(it), next(it)
    if emit_vn:
        vn_ref = next(it)
    kwin, vwin, cbuf, ybuf = next(it), next(it), next(it), next(it)

    t = pl.program_id(1)
    W = BRANCH_W
    T = n_seq * Ts
    hist = KPAD + WINDOW
    seq_chunks = Ts // CHUNK
    n_chunks = n_seq * seq_chunks
    base = CONV_OFF - CONV_HIST
    seq_rows = [slice(s * Ts, (s + 1) * Ts) for s in range(n_seq)]

    def rows_of(ref):
        return ref[0] if n_seq == 1 else jnp.concatenate([ref[s] for s in range(n_seq)], axis=0)

    @pl.when(t == 0)
    def _init():
        for s in range(n_seq):
            if has_cache:
                kwin[s, 0:KPAD, :] = jnp.zeros((KPAD, W), BF16)
                vwin[s, 0:KPAD, :] = jnp.zeros((KPAD, W), BF16)
                kwin[s, KPAD:hist, :] = ck_ref[s].astype(BF16)
                vwin[s, KPAD:hist, :] = cv_ref[s].astype(BF16)
                cbuf[s, 0:CONV_OFF, :] = jnp.zeros((CONV_OFF, W), F32)
                cbuf[s, base:CONV_OFF, :] = cc_ref[s]
            else:
                kwin[s, 0:hist, :] = jnp.zeros((hist, W), BF16)
                vwin[s, 0:hist, :] = jnp.zeros((hist, W), BF16)
                cbuf[s, 0:CONV_OFF, :] = jnp.zeros((CONV_OFF, W), F32)

    h = _rms(rows_of(x_ref), gmix_ref[...]).astype(BF16)

    def sec(lo, width):
        return jnp.dot(h, win_ref[:, lo:lo + width], preferred_element_type=F32)

    def gate_pre(n):
        return sec(7 * W + n * D_MODEL, D_MODEL) + bgate_ref[:, n * D_MODEL:(n + 1) * D_MODEL]

    conv_acc = [None] * n_seq

    def add_conv_taps(*residues):
        for s in range(n_seq):
            for r in residues:
                part = None
                for off in range(base, base + CONV_W):
                    if off % SUBLANES == r:
                        term = cbuf[s, off:off + Ts, :] * cdw_ref[off - base:off - base + 1, :]
                        part = term if part is None else part + term
                conv_acc[s] = part if conv_acc[s] is None else conv_acc[s] + part

    low = lax.broadcasted_iota(jnp.int32, (T, LANES), 1) < HEAD_DIM

    def pair_unit(zp):
        sq = zp * zp
        s0 = jnp.sum(jnp.where(low, sq, 0.0), axis=1, keepdims=True)
        s1 = jnp.sum(jnp.where(low, 0.0, sq), axis=1, keepdims=True)
        r = jnp.where(low, lax.rsqrt(s0 * (1.0 / HEAD_DIM) + EPS), lax.rsqrt(s1 * (1.0 / HEAD_DIM) + EPS))
        return zp * r

    def band(win, c, p):
        s, lo = c // seq_chunks, (c % seq_chunks) * CHUNK
        return win[s, lo:lo + BAND_PAD, p * LANES:(p + 1) * LANES]

    def scores(c):
        rows = slice(c * CHUNK, (c + 1) * CHUNK)
        out = []
        for p in range(N_PAIRS):
            q2 = jnp.concatenate([q_lo[p][rows], q_hi[p][rows]], axis=0)
            kb = band(kwin, c, p)
            out.append(lax.dot_general(q2, kb, NT_DIMS, preferred_element_type=F32))
        return out

    col = lax.broadcasted_iota(jnp.int32, (2 * CHUNK, BAND_PAD), 1)
    low_c = lax.broadcasted_iota(jnp.int32, (CHUNK, LANES), 1) < HEAD_DIM

    def softmax(c, s_pairs):
        if not has_cache:
            valid = col >= (hist - t * Ts - (c % seq_chunks) * CHUNK)
        out = []
        for p in range(N_PAIRS):
            s = s_pairs[p] + relb_ref[p]
            if not has_cache:
                s = jnp.where(valid, s, NEG_INF)
            e = jnp.exp2(s - jnp.max(s, axis=1, keepdims=True))
            out.append((e.astype(BF16), jnp.sum(e, axis=1, keepdims=True)))
        return out

    def pv(c, e_pairs):
        for p in range(N_PAIRS):
            sl = slice(p * LANES, (p + 1) * LANES)
            e, den = e_pairs[p]
            vb = band(vwin, c, p)
            o2 = jnp.dot(e, vb, preferred_element_type=F32) / den
            ybuf[c * CHUNK:(c + 1) * CHUNK, sl] = jnp.where(low_c, o2[0:CHUNK], o2[CHUNK:2 * CHUNK]).astype(BF16)

    z_a = sec(2 * W, W)
    z_b = sec(3 * W, W)
    glu = z_a * jax.nn.sigmoid(z_b)
    for s in range(n_seq):
        cbuf[s, CONV_OFF:CONV_OFF + Ts, :] = glu[seq_rows[s]]
    k = sec(5 * W, W)
    add_conv_taps(0, 1)
    va = sec(6 * W, W)
    add_conv_taps(2, 3)
    for p in range(N_PAIRS):
        sl = slice(p * LANES, (p + 1) * LANES)
        k_n = pair_unit(k[:, sl]) * gk_ref[:, sl]
        for s in range(n_seq):
            kst_ref[s, :, sl] = k_n[seq_rows[s]]
            kwin[s, hist:hist + Ts, sl] = k_n[seq_rows[s]].astype(BF16)
    q = sec(4 * W, W)
    add_conv_taps(4, 5)
    for s in range(n_seq):
        vst_ref[s] = va[seq_rows[s]]
        vwin[s, hist:hist + Ts, :] = va[seq_rows[s]].astype(BF16)
    z_u = sec(0, W)
    add_conv_taps(6, 7)
    q_lo, q_hi = [], []
    for p in range(N_PAIRS):
        sl = slice(p * LANES, (p + 1) * LANES)
        q_unit = pair_unit(q[:, sl])
        q_lo.append((q_unit * gqlo_ref[:, sl]).astype(BF16))
        q_hi.append((q_unit * gqhi_ref[:, sl]).astype(BF16))
    z_v = sec(W, W)
    conv = conv_acc[0] if n_seq == 1 else jnp.concatenate(conv_acc, axis=0)
    y_b = jax.nn.silu(_rms(conv + cb_ref[...], gconv_ref[...])).astype(BF16)
    for s in range(n_seq):
        cst_ref[s] = cbuf[s, base + Ts:base + Ts + CONV_HIST, :]
        cbuf[s, 0:CONV_OFF, :] = cbuf[s, Ts:Ts + CONV_OFF, :]
    u = _gelu(z_u)
    g_pre0 = gate_pre(0)
    v_n = _rms(_gelu(z_v), ggmlp_ref[...])
    if emit_vn:
        for s in range(n_seq):
            vn_ref[s] = v_n[seq_rows[s]]
    v_nb = v_n.astype(BF16)
    g_pre1 = gate_pre(1)
    gate0 = jax.nn.sigmoid(g_pre0)
    g_pre2 = gate_pre(2)
    gate1 = jax.nn.sigmoid(g_pre1)

    s_pairs = scores(0)
    gate2 = jax.nn.sigmoid(g_pre2)
    r_i = lax.broadcasted_iota(jnp.int32, (GMLP_GROUPS, gc, gc), 1)
    c_i = lax.broadcasted_iota(jnp.int32, (GMLP_GROUPS, gc, gc), 2)
    wm = jnp.where(r_i >= c_i, ws_ref[:, 0:gc, 0:gc], 0.0).astype(BF16)
    row_blocks = []
    for n in range(T // gc):
        cols = [jnp.dot(wm[g], v_nb[n * gc:(n + 1) * gc, g * GMLP_GW:(g + 1) * GMLP_GW],
                        preferred_element_type=F32) for g in range(GMLP_GROUPS)]
        row_blocks.append(jnp.concatenate(cols, axis=1) + gbias_ref[0:gc, :])
    mixed = row_blocks[0] if len(row_blocks) == 1 else jnp.concatenate(row_blocks, axis=0)
    y_a = (u * mixed).astype(BF16)
    merged = None
    for c in range(n_chunks):
        s_next = scores(c + 1) if c + 1 < n_chunks else None
        e_pairs = softmax(c, s_pairs)
        if c == 0:
            merged = gate0 * jnp.dot(y_a, wbr_ref[0], preferred_element_type=F32)
        if c == min(1, n_chunks - 1):
            merged = merged + gate1 * jnp.dot(y_b, wbr_ref[1], preferred_element_type=F32)
        pv(c, e_pairs)
        s_pairs = s_next

    merged = merged + gate2 * jnp.dot(ybuf[...], wbr_ref[2], preferred_element_type=F32)
    x_out = rows_of(x_ref) + jnp.dot(merged.astype(BF16), wout_ref[...], preferred_element_type=F32)
    for s in range(n_seq):
        xo_ref[s] = x_out[seq_rows[s]]

    if carry:
        for s in range(n_seq):
            kwin[s, KPAD:hist, :] = kwin[s, KPAD + Ts:hist + Ts, :]
            vwin[s, KPAD:hist, :] = vwin[s, KPAD + Ts:hist + Ts, :]


def _layer_spec(a, layer):
    zeros = (0,) * (a.ndim - 1)
    return pl.BlockSpec((None,) + a.shape[1:], lambda *g: (layer,) + zeros, pipeline_mode=pl.Buffered(1))


def _mixer(x, cache, p, wl, layer, *, Ts, n_seq, emit_vn):
    B, L, _ = x.shape
    W = BRANCH_W
    nT = L // Ts
    keep = min(WINDOW, L)
    nkeep = keep // Ts
    gc = min(L, GMLP_CHUNK)
    has_cache = cache is not None
    assert L % Ts == 0 and Ts % CHUNK == 0 and Ts % gc == 0 and keep % Ts == 0 and Ts <= WINDOW and B % n_seq == 0

    tile = lambda b, t: (b, t, 0)
    per_b = lambda b, t: (b, 0, 0)
    state = lambda b, t: (b, jnp.maximum(t - (nT - nkeep), 0), 0)

    args = [x]
    in_specs = [pl.BlockSpec((n_seq, Ts, D_MODEL), tile)]
    if has_cache:
        ck, cv, cc = cache
        args += [ck.reshape(B, WINDOW, W), cv.reshape(B, WINDOW, W), cc]
        in_specs += [pl.BlockSpec((n_seq, WINDOW, W), per_b), pl.BlockSpec((n_seq, WINDOW, W), per_b),
                     pl.BlockSpec((n_seq, CONV_HIST, W), per_b)]
    for name in ("g_mix", "w_in", "b_gate", "g_gmlp", "ws", "gbias", "conv_dw", "conv_b",
                 "g_conv", "gq_lo", "gq_hi", "g_k", "relb", "w_branch", "w_out"):
        a, index = (wl[name], 0) if name in wl else (p[name], layer)
        args.append(a)
        in_specs.append(_layer_spec(a, index))

    out_shape = [jax.ShapeDtypeStruct((B, L, D_MODEL), F32),
                 jax.ShapeDtypeStruct((B, keep, W), F32),
                 jax.ShapeDtypeStruct((B, keep, W), F32),
                 jax.ShapeDtypeStruct((B, CONV_HIST, W), F32)]
    out_specs = [pl.BlockSpec((n_seq, Ts, D_MODEL), tile),
                 pl.BlockSpec((n_seq, Ts, W), state),
                 pl.BlockSpec((n_seq, Ts, W), state),
                 pl.BlockSpec((n_seq, CONV_HIST, W), per_b)]
    if emit_vn:
        out_shape.append(jax.ShapeDtypeStruct((B, L, W), F32))
        out_specs.append(pl.BlockSpec((n_seq, Ts, W), tile))

    win_rows = KPAD + WINDOW + Ts
    return pl.pallas_call(
        functools.partial(_mixer_kernel, n_seq=n_seq, Ts=Ts, has_cache=has_cache, emit_vn=emit_vn, gc=gc,
                          carry=nT > 1),
        grid=(B // n_seq, nT),
        in_specs=in_specs,
        out_specs=out_specs,
        out_shape=out_shape,
        scratch_shapes=[pltpu.VMEM((n_seq, win_rows, W), BF16), pltpu.VMEM((n_seq, win_rows, W), BF16),
                        pltpu.VMEM((n_seq, CONV_OFF + Ts, W), F32), pltpu.VMEM((n_seq * Ts, W), BF16)],
        compiler_params=pltpu.CompilerParams(dimension_semantics=("arbitrary", "arbitrary"),
                                             vmem_limit_bytes=VMEM_LIMIT),
        name="mixer_cache" if has_cache else "mixer_prompt",
    )(*args)


def _ffn_kernel(x_ref, g_ref, wgu_ref, wd_ref, *rest):
    n_cast = len(rest) // 2
    o_ref = rest[n_cast]
    for src, dst in zip(rest[:n_cast], rest[n_cast + 1:]):
        dst[...] = src[...].astype(BF16)
    x = x_ref[...]
    h2 = _rms(x, g_ref[...]).astype(BF16)
    acc = x
    for j in range(FFN_HIDDEN // FFN_HC):
        lo = j * FFN_HC
        g = jnp.dot(h2, wgu_ref[:, lo:lo + FFN_HC], preferred_element_type=F32)
        u = jnp.dot(h2, wgu_ref[:, FFN_HIDDEN + lo:FFN_HIDDEN + lo + FFN_HC], preferred_element_type=F32)
        a = (jax.nn.silu(g) * u).astype(BF16)
        acc = acc + jnp.dot(a, wd_ref[lo:lo + FFN_HC, :], preferred_element_type=F32)
    o_ref[...] = acc


def _ffn(x, p, wl, layer, *, T, cast_next=None):
    B, L, _ = x.shape
    rows = B * L
    n_steps = rows // T
    assert rows % T == 0 and FFN_HIDDEN % FFN_HC == 0
    tile = pl.BlockSpec((T, D_MODEL), lambda i: (i, 0))
    args = [x.reshape(rows, D_MODEL), p["g_ffn"], wl["w_gate_up"], wl["w_down"]]
    in_specs = [tile, _layer_spec(p["g_ffn"], layer), _layer_spec(wl["w_gate_up"], 0), _layer_spec(wl["w_down"], 0)]
    out_shape = [jax.ShapeDtypeStruct((rows, D_MODEL), F32)]
    out_specs = [tile]
    names = []
    if cast_next is not None:
        stacked, nxt = cast_next
        for name, w in stacked.items():
            depth, r, c = w.shape[0], math.prod(w.shape[1:-1]), w.shape[-1]
            n_c = max(d for d in range(1, n_steps + 1) if n_steps % d == 0 and r % (d * BF16_ROWS) == 0)
            every = n_steps // n_c
            names.append(name)
            args.append(w.reshape(depth, n_c, r // n_c, c))
            in_specs.append(pl.BlockSpec((None, None, r // n_c, c), lambda i, every=every: (nxt, i // every, 0, 0)))
            out_shape.append(jax.ShapeDtypeStruct((n_c, r // n_c, c), BF16))
            out_specs.append(pl.BlockSpec((None, r // n_c, c), lambda i, every=every: (i // every, 0, 0)))
    outs = pl.pallas_call(
        _ffn_kernel,
        grid=(n_steps,),
        in_specs=in_specs,
        out_specs=out_specs,
        out_shape=out_shape,
        compiler_params=pltpu.CompilerParams(dimension_semantics=("arbitrary",),
                                             vmem_limit_bytes=VMEM_LIMIT),
        name="ffn",
    )(*args)
    wl_next = {name: o.reshape((1,) + cast_next[0][name].shape[1:]) for name, o in zip(names, outs[1:])}
    return outs[0].reshape(B, L, D_MODEL), wl_next


def _rel_bias_table(rel_bias):
    n_heads = rel_bias.shape[0]
    n_far = BAND_PAD - MAX_REL
    period = BAND_PAD + CHUNK
    far = jnp.broadcast_to(rel_bias[:, 2 * MAX_REL:2 * MAX_REL + 1], (n_heads, n_far))
    near = jnp.flip(rel_bias[:, 2 * MAX_REL - (period - n_far):2 * MAX_REL], axis=1)
    g = jnp.concatenate([far, near], axis=1) * LOG2E
    tbl = jnp.stack([g[:, CHUNK - 1 - q:CHUNK - 1 - q + BAND_PAD] for q in range(CHUNK)], axis=1)
    tbl = jnp.where(jnp.arange(BAND_PAD)[None, None, :] >= KPAD, tbl, NEG_INF)
    return tbl.reshape(n_heads // 2, 2 * CHUNK, BAND_PAD).astype(F32)


def kernel(x_prompt, x_sample, cache_attn_k, cache_attn_v, cache_conv, norm_mix_g, w_in, b_gate, gmlp_norm_g, gmlp_ws, gmlp_bs, conv_dw, conv_b, conv_norm_g, q_norm_g, k_norm_g, rel_bias, w_branch, w_out, norm_ffn_g, w_gate_up, w_down):
    depth = w_in.shape[0]
    xp, xs = x_prompt, x_sample
    Bp, Bs, Ls = xp.shape[0], xs.shape[0], xs.shape[1]
    rows = lambda a: a.reshape(depth, 1, -1)
    lane_low = (jnp.arange(BRANCH_W) % LANES) < HEAD_DIM
    gq = jnp.tile(q_norm_g, (1, N_HEADS)) * (ATTN_SCALE * LOG2E)
    weights = {"w_in": w_in, "w_branch": w_branch, "w_out": w_out, "w_gate_up": w_gate_up, "w_down": w_down}
    wl = {name: w[0:1].astype(BF16) for name, w in weights.items()}
    p = {
        "g_mix": rows(norm_mix_g), "b_gate": rows(b_gate), "g_gmlp": rows(gmlp_norm_g), "ws": gmlp_ws,
        "gbias": jnp.repeat(jnp.swapaxes(gmlp_bs, 1, 2), GMLP_GW, axis=2),
        "conv_dw": conv_dw, "conv_b": rows(conv_b), "g_conv": rows(conv_norm_g),
        "gq_lo": rows(jnp.where(lane_low, gq, 0.0)), "gq_hi": rows(jnp.where(lane_low, 0.0, gq)),
        "g_k": rows(jnp.tile(k_norm_g, (1, N_HEADS))),
        "relb": _rel_bias_table(rel_bias.reshape(depth * N_HEADS, -1)).reshape(depth, N_PAIRS, 2 * CHUNK, BAND_PAD),
        "g_ffn": rows(norm_ffn_g),
    }
    kp_l, vp_l, cp_l, ks_l, vs_l, cs_l, gs_l = [], [], [], [], [], [], []
    for l in range(depth):
        xp, kp, vp, cp = _mixer(xp, None, p, wl, l, Ts=MIXER_TILE, n_seq=1, emit_vn=False)
        xp, wl_next = _ffn(xp, p, wl, l, T=FFN_TILE, cast_next=(weights, l + 1) if l + 1 < depth else None)
        xs, ks, vs, cs, gs = _mixer(xs, (cache_attn_k[l], cache_attn_v[l], cache_conv[l]), p, wl, l, Ts=Ls,
                                    n_seq=SAMPLE_SEQS, emit_vn=True)
        xs, _ = _ffn(xs, p, wl, l, T=Bs * Ls)
        wl = wl_next
        kp_l.append(kp.reshape(Bp, -1, N_HEADS, HEAD_DIM))
        vp_l.append(vp.reshape(Bp, -1, N_HEADS, HEAD_DIM))
        cp_l.append(cp)
        ks_l.append(ks.reshape(Bs, Ls, N_HEADS, HEAD_DIM))
        vs_l.append(vs.reshape(Bs, Ls, N_HEADS, HEAD_DIM))
        cs_l.append(cs)
        gs_l.append(gs)
    return (xp, xs, jnp.stack(kp_l), jnp.stack(vp_l), jnp.stack(cp_l),
            jnp.stack(ks_l), jnp.stack(vs_l), jnp.stack(cs_l), jnp.stack(gs_l))
```

```python
import functools
import math

import jax
import jax.numpy as jnp
from jax import lax
from jax.experimental import pallas as pl
from jax.experimental.pallas import tpu as pltpu

D_MODEL = 1024
BRANCH_W = 512
N_BRANCH = 3
GMLP_CHUNK = 128
GMLP_GROUPS = 4
GMLP_GW = BRANCH_W // GMLP_GROUPS
CONV_W = 31
N_HEADS = 8
HEAD_DIM = 64
CHUNK = 64
WINDOW = 512
BAND = WINDOW + CHUNK
MAX_REL = 128
FFN_HIDDEN = 2816
IN_COLS = 7 * BRANCH_W + N_BRANCH * D_MODEL
ATTN_SCALE = HEAD_DIM ** -0.5
NEG_INF = -1e30
EPS = 1e-6

LANES = 128
SUBLANES = 8
BF16_ROWS = 16
N_PAIRS = N_HEADS // 2
KPAD = CHUNK
BAND_PAD = BAND + KPAD
CONV_HIST = CONV_W - 1
CONV_OFF = 32
VMEM_LIMIT = 56 * 1024 * 1024
MIXER_TILE = 512
SAMPLE_SEQS = 2
FFN_TILE = 1024
FFN_HC = 256

F32 = jnp.float32
BF16 = jnp.bfloat16
NT_DIMS = (((1,), (1,)), ((), ()))

GELU_A = 2.0 * math.sqrt(2.0 / math.pi)
GELU_B = GELU_A * 0.044715
LOG2E = math.log2(math.e)


def _rms(x, g):
    return x * lax.rsqrt(jnp.mean(x * x, axis=-1, keepdims=True) + EPS) * g


def _gelu(x):
    return x * jax.nn.sigmoid(x * (GELU_A + GELU_B * (x * x)))


def _mixer_kernel(*refs, n_seq, Ts, has_cache, emit_vn, gc, carry):
    it = iter(refs)
    x_ref = next(it)
    if has_cache:
        ck_ref, cv_ref, cc_ref = next(it), next(it), next(it)
    gmix_ref, win_ref, bgate_ref = next(it), next(it), next(it)
    ggmlp_ref, ws_ref, gbias_ref = next(it), next(it), next(it)
    cdw_ref, cb_ref, gconv_ref = next(it), next(it), next(it)
    gqlo_ref, gqhi_ref, gk_ref, relb_ref = next(it), next(it), next(it), next(it)
    wbr_ref, wout_ref = next(it), next(it)
    xo_ref, kst_ref, vst_ref, cst_ref = next(it), next(it), next(it), next(it)
    if emit_vn:
        vn_ref = next(it)
    kwin, vwin, cbuf, ybuf = next(it), next(it), next(it), next(it)

    t = pl.program_id(1)
    W = BRANCH_W
    T = n_seq * Ts
    hist = KPAD + WINDOW
    seq_chunks = Ts // CHUNK
    n_chunks = n_seq * seq_chunks
    base = CONV_OFF - CONV_HIST
    seq_rows = [slice(s * Ts, (s + 1) * Ts) for s in range(n_seq)]

    def rows_of(ref):
        return ref[0] if n_seq == 1 else jnp.concatenate([ref[s] for s in range(n_seq)], axis=0)

    @pl.when(t == 0)
    def _init():
        for s in range(n_seq):
            if has_cache:
                kwin[s, 0:KPAD, :] = jnp.zeros((KPAD, W), BF16)
                vwin[s, 0:KPAD, :] = jnp.zeros((KPAD, W), BF16)
                kwin[s, KPAD:hist, :] = ck_ref[s].astype(BF16)
                vwin[s, KPAD:hist, :] = cv_ref[s].astype(BF16)
                cbuf[s, 0:CONV_OFF, :] = jnp.zeros((CONV_OFF, W), F32)
                cbuf[s, base:CONV_OFF, :] = cc_ref[s]
            else:
                kwin[s, 0:hist, :] = jnp.zeros((hist, W), BF16)
                vwin[s, 0:hist, :] = jnp.zeros((hist, W), BF16)
                cbuf[s, 0:CONV_OFF, :] = jnp.zeros((CONV_OFF, W), F32)

    h = _rms(rows_of(x_ref), gmix_ref[...]).astype(BF16)

    def sec(lo, width):
        return jnp.dot(h, win_ref[:, lo:lo + width], preferred_element_type=F32)

    def gate_pre(n):
        return sec(7 * W + n * D_MODEL, D_MODEL) + bgate_ref[:, n * D_MODEL:(n + 1) * D_MODEL]

    conv_acc = [None] * n_seq

    def add_conv_taps(*residues):
        for s in range(n_seq):
            for r in residues:
                part = None
                for off in range(base, base + CONV_W):
                    if off % SUBLANES == r:
                        term = cbuf[s, off:off + Ts, :] * cdw_ref[off - base:off - base + 1, :]
                        part = term if part is None else part + term
                conv_acc[s] = part if conv_acc[s] is None else conv_acc[s] + part

    low = lax.broadcasted_iota(jnp.int32, (T, LANES), 1) < HEAD_DIM

    def pair_unit(zp):
        sq = zp * zp
        s0 = jnp.sum(jnp.where(low, sq, 0.0), axis=1, keepdims=True)
        s1 = jnp.sum(jnp.where(low, 0.0, sq), axis=1, keepdims=True)
        r = jnp.where(low, lax.rsqrt(s0 * (1.0 / HEAD_DIM) + EPS), lax.rsqrt(s1 * (1.0 / HEAD_DIM) + EPS))
        return zp * r

    def band(win, c, p):
        s, lo = c // seq_chunks, (c % seq_chunks) * CHUNK
        return win[s, lo:lo + BAND_PAD, p * LANES:(p + 1) * LANES]

    def scores(c):
        rows = slice(c * CHUNK, (c + 1) * CHUNK)
        out = []
        for p in range(N_PAIRS):
            q2 = jnp.concatenate([q_lo[p][rows], q_hi[p][rows]], axis=0)
            kb = band(kwin, c, p)
            out.append(lax.dot_general(q2, kb, NT_DIMS, preferred_element_type=F32))
        return out

    col = lax.broadcasted_iota(jnp.int32, (2 * CHUNK, BAND_PAD), 1)
    low_c = lax.broadcasted_iota(jnp.int32, (CHUNK, LANES), 1) < HEAD_DIM

    def softmax(c, s_pairs):
        if not has_cache:
            valid = col >= (hist - t * Ts - (c % seq_chunks) * CHUNK)
        out = []
        for p in range(N_PAIRS):
            s = s_pairs[p] + relb_ref[p]
            if not has_cache:
                s = jnp.where(valid, s, NEG_INF)
            e = jnp.exp2(s - jnp.max(s, axis=1, keepdims=True))
            out.append((e.astype(BF16), jnp.sum(e, axis=1, keepdims=True)))
        return out

    def pv(c, e_pairs):
        for p in range(N_PAIRS):
            sl = slice(p * LANES, (p + 1) * LANES)
            e, den = e_pairs[p]
            vb = band(vwin, c, p)
            o2 = jnp.dot(e, vb, preferred_element_type=F32) / den
            ybuf[c * CHUNK:(c + 1) * CHUNK, sl] = jnp.where(low_c, o2[0:CHUNK], o2[CHUNK:2 * CHUNK]).astype(BF16)

    z_a = sec(2 * W, W)
    z_b = sec(3 * W, W)
    glu = z_a * jax.nn.sigmoid(z_b)
    for s in range(n_seq):
        cbuf[s, CONV_OFF:CONV_OFF + Ts, :] = glu[seq_rows[s]]
    k = sec(5 * W, W)
    add_conv_taps(0, 1)
    va = sec(6 * W, W)
    add_conv_taps(2, 3)
    for p in range(N_PAIRS):
        sl = slice(p * LANES, (p + 1) * LANES)
        k_n = pair_unit(k[:, sl]) * gk_ref[:, sl]
        for s in range(n_seq):
            kst_ref[s, :, sl] = k_n[seq_rows[s]]
            kwin[s, hist:hist + Ts, sl] = k_n[seq_rows[s]].astype(BF16)
    q = sec(4 * W, W)
    add_conv_taps(4, 5)
    for s in range(n_seq):
        vst_ref[s] = va[seq_rows[s]]
        vwin[s, hist:hist + Ts, :] = va[seq_rows[s]].astype(BF16)
    z_u = sec(0, W)
    add_conv_taps(6, 7)
    q_lo, q_hi = [], []
    for p in range(N_PAIRS):
        sl = slice(p * LANES, (p + 1) * LANES)
        q_unit = pair_unit(q[:, sl])
        q_lo.append((q_unit * gqlo_ref[:, sl]).astype(BF16))
        q_hi.append((q_unit * gqhi_ref[:, sl]).astype(BF16))
    z_v = sec(W, W)
    conv = conv_acc[0] if n_seq == 1 else jnp.concatenate(conv_acc, axis=0)
    y_b = jax.nn.silu(_rms(conv + cb_ref[...], gconv_ref[...])).astype(BF16)
    for s in range(n_seq):
        cst_ref[s] = cbuf[s, base + Ts:base + Ts + CONV_HIST, :]
        cbuf[s, 0:CONV_OFF, :] = cbuf[s, Ts:Ts + CONV_OFF, :]
    u = _gelu(z_u)
    g_pre0 = gate_pre(0)
    v_n = _rms(_gelu(z_v), ggmlp_ref[...])
    if emit_vn:
        for s in range(n_seq):
            vn_ref[s] = v_n[seq_rows[s]]
    v_nb = v_n.astype(BF16)
    g_pre1 = gate_pre(1)
    gate0 = jax.nn.sigmoid(g_pre0)
    g_pre2 = gate_pre(2)
    gate1 = jax.nn.sigmoid(g_pre1)

    s_pairs = scores(0)
    gate2 = jax.nn.sigmoid(g_pre2)
    r_i = lax.broadcasted_iota(jnp.int32, (GMLP_GROUPS, gc, gc), 1)
    c_i = lax.broadcasted_iota(jnp.int32, (GMLP_GROUPS, gc, gc), 2)
    wm = jnp.where(r_i >= c_i, ws_ref[:, 0:gc, 0:gc], 0.0).astype(BF16)
    row_blocks = []
    for n in range(T // gc):
        cols = [jnp.dot(wm[g], v_nb[n * gc:(n + 1) * gc, g * GMLP_GW:(g + 1) * GMLP_GW],
                        preferred_element_type=F32) for g in range(GMLP_GROUPS)]
        row_blocks.append(jnp.concatenate(cols, axis=1) + gbias_ref[0:gc, :])
    mixed = row_blocks[0] if len(row_blocks) == 1 else jnp.concatenate(row_blocks, axis=0)
    y_a = (u * mixed).astype(BF16)
    merged = None
    for c in range(n_chunks):
        s_next = scores(c + 1) if c + 1 < n_chunks else None
        e_pairs = softmax(c, s_pairs)
        if c == 0:
            merged = gate0 * jnp.dot(y_a, wbr_ref[0], preferred_element_type=F32)
        if c == min(1, n_chunks - 1):
            merged = merged + gate1 * jnp.dot(y_b, wbr_ref[1], preferred_element_type=F32)
        pv(c, e_pairs)
        s_pairs = s_next

    merged = merged + gate2 * jnp.dot(ybuf[...], wbr_ref[2], preferred_element_type=F32)
    x_out = rows_of(x_ref) + jnp.dot(merged.astype(BF16), wout_ref[...], preferred_element_type=F32)
    for s in range(n_seq):
        xo_ref[s] = x_out[seq_rows[s]]

    if carry:
        for s in range(n_seq):
            kwin[s, KPAD:hist, :] = kwin[s, KPAD + Ts:hist + Ts, :]
            vwin[s, KPAD:hist, :] = vwin[s, KPAD + Ts:hist + Ts, :]


def _layer_spec(a, layer):
    zeros = (0,) * (a.ndim - 1)
    return pl.BlockSpec((None,) + a.shape[1:], lambda *g: (layer,) + zeros, pipeline_mode=pl.Buffered(1))


def _mixer(x, cache, p, wl, layer, *, Ts, n_seq, emit_vn):
    B, L, _ = x.shape
    W = BRANCH_W
    nT = L // Ts
    keep = min(WINDOW, L)
    nkeep = keep // Ts
    gc = min(L, GMLP_CHUNK)
    has_cache = cache is not None
    assert L % Ts == 0 and Ts % CHUNK == 0 and Ts % gc == 0 and keep % Ts == 0 and Ts <= WINDOW and B % n_seq == 0

    tile = lambda b, t: (b, t, 0)
    per_b = lambda b, t: (b, 0, 0)
    state = lambda b, t: (b, jnp.maximum(t - (nT - nkeep), 0), 0)

    args = [x]
    in_specs = [pl.BlockSpec((n_seq, Ts, D_MODEL), tile)]
    if has_cache:
        ck, cv, cc = cache
        args += [ck.reshape(B, WINDOW, W), cv.reshape(B, WINDOW, W), cc]
        in_specs += [pl.BlockSpec((n_seq, WINDOW, W), per_b), pl.BlockSpec((n_seq, WINDOW, W), per_b),
                     pl.BlockSpec((n_seq, CONV_HIST, W), per_b)]
    for name in ("g_mix", "w_in", "b_gate", "g_gmlp", "ws", "gbias", "conv_dw", "conv_b",
                 "g_conv", "gq_lo", "gq_hi", "g_k", "relb", "w_branch", "w_out"):
        a, index = (wl[name], 0) if name in wl else (p[name], layer)
        args.append(a)
        in_specs.append(_layer_spec(a, index))

    out_shape = [jax.ShapeDtypeStruct((B, L, D_MODEL), F32),
                 jax.ShapeDtypeStruct((B, keep, W), F32),
                 jax.ShapeDtypeStruct((B, keep, W), F32),
                 jax.ShapeDtypeStruct((B, CONV_HIST, W), F32)]
    out_specs = [pl.BlockSpec((n_seq, Ts, D_MODEL), tile),
                 pl.BlockSpec((n_seq, Ts, W), state),
                 pl.BlockSpec((n_seq, Ts, W), state),
                 pl.BlockSpec((n_seq, CONV_HIST, W), per_b)]
    if emit_vn:
        out_shape.append(jax.ShapeDtypeStruct((B, L, W), F32))
        out_specs.append(pl.BlockSpec((n_seq, Ts, W), tile))

    win_rows = KPAD + WINDOW + Ts
    return pl.pallas_call(
        functools.partial(_mixer_kernel, n_seq=n_seq, Ts=Ts, has_cache=has_cache, emit_vn=emit_vn, gc=gc,
                          carry=nT > 1),
        grid=(B // n_seq, nT),
        in_specs=in_specs,
        out_specs=out_specs,
        out_shape=out_shape,
        scratch_shapes=[pltpu.VMEM((n_seq, win_rows, W), BF16), pltpu.VMEM((n_seq, win_rows, W), BF16),
                        pltpu.VMEM((n_seq, CONV_OFF + Ts, W), F32), pltpu.VMEM((n_seq * Ts, W), BF16)],
        compiler_params=pltpu.CompilerParams(dimension_semantics=("arbitrary", "arbitrary"),
                                             vmem_limit_bytes=VMEM_LIMIT),
        name="mixer_cache" if has_cache else "mixer_prompt",
    )(*args)


def _ffn_kernel(x_ref, g_ref, wgu_ref, wd_ref, *rest):
    n_cast = len(rest) // 2
    o_ref = rest[n_cast]
    for src, dst in zip(rest[:n_cast], rest[n_cast + 1:]):
        dst[...] = src[...].astype(BF16)
    x = x_ref[...]
    h2 = _rms(x, g_ref[...]).astype(BF16)
    acc = x
    for j in range(FFN_HIDDEN // FFN_HC):
        lo = j * FFN_HC
        g = jnp.dot(h2, wgu_ref[:, lo:lo + FFN_HC], preferred_element_type=F32)
        u = jnp.dot(h2, wgu_ref[:, FFN_HIDDEN + lo:FFN_HIDDEN + lo + FFN_HC], preferred_element_type=F32)
        a = (jax.nn.silu(g) * u).astype(BF16)
        acc = acc + jnp.dot(a, wd_ref[lo:lo + FFN_HC, :], preferred_element_type=F32)
    o_ref[...] = acc


def _ffn(x, p, wl, layer, *, T, cast_next=None):
    B, L, _ = x.shape
    rows = B * L
    n_steps = rows // T
    assert rows % T == 0 and FFN_HIDDEN % FFN_HC == 0
    tile = pl.BlockSpec((T, D_MODEL), lambda i: (i, 0))
    args = [x.reshape(rows, D_MODEL), p["g_ffn"], wl["w_gate_up"], wl["w_down"]]
    in_specs = [tile, _layer_spec(p["g_ffn"], layer), _layer_spec(wl["w_gate_up"], 0), _layer_spec(wl["w_down"], 0)]
    out_shape = [jax.ShapeDtypeStruct((rows, D_MODEL), F32)]
    out_specs = [tile]
    names = []
    if cast_next is not None:
        stacked, nxt = cast_next
        for name, w in stacked.items():
            depth, r, c = w.shape[0], math.prod(w.shape[1:-1]), w.shape[-1]
            n_c = max(d for d in range(1, n_steps + 1) if n_steps % d == 0 and r % (d * BF16_ROWS) == 0)
            every = n_steps // n_c
            names.append(name)
            args.append(w.reshape(depth, n_c, r // n_c, c))
            in_specs.append(pl.BlockSpec((None, None, r // n_c, c), lambda i, every=every: (nxt, i // every, 0, 0)))
            out_shape.append(jax.ShapeDtypeStruct((n_c, r // n_c, c), BF16))
            out_specs.append(pl.BlockSpec((None, r // n_c, c), lambda i, every=every: (i // every, 0, 0)))
    outs = pl.pallas_call(
        _ffn_kernel,
        grid=(n_steps,),
        in_specs=in_specs,
        out_specs=out_specs,
        out_shape=out_shape,
        compiler_params=pltpu.CompilerParams(dimension_semantics=("arbitrary",),
                                             vmem_limit_bytes=VMEM_LIMIT),
        name="ffn",
    )(*args)
    wl_next = {name: o.reshape((1,) + cast_next[0][name].shape[1:]) for name, o in zip(names, outs[1:])}
    return outs[0].reshape(B, L, D_MODEL), wl_next


def _rel_bias_table(rel_bias):
    n_heads = rel_bias.shape[0]
    n_far = BAND_PAD - MAX_REL
    period = BAND_PAD + CHUNK
    far = jnp.broadcast_to(rel_bias[:, 2 * MAX_REL:2 * MAX_REL + 1], (n_heads, n_far))
    near = jnp.flip(rel_bias[:, 2 * MAX_REL - (period - n_far):2 * MAX_REL], axis=1)
    g = jnp.concatenate([far, near], axis=1) * LOG2E
    skew = jnp.tile(g, (1, CHUNK))[:, :CHUNK * (period - 1)].reshape(n_heads, CHUNK, period - 1)
    tbl = skew[:, :, CHUNK - 1:CHUNK - 1 + BAND_PAD]
    tbl = jnp.where(jnp.arange(BAND_PAD)[None, None, :] >= KPAD, tbl, NEG_INF)
    return tbl.reshape(n_heads // 2, 2 * CHUNK, BAND_PAD).astype(F32)


def kernel(x_prompt, x_sample, cache_attn_k, cache_attn_v, cache_conv, norm_mix_g, w_in, b_gate, gmlp_norm_g, gmlp_ws, gmlp_bs, conv_dw, conv_b, conv_norm_g, q_norm_g, k_norm_g, rel_bias, w_branch, w_out, norm_ffn_g, w_gate_up, w_down):
    depth = w_in.shape[0]
    xp, xs = x_prompt, x_sample
    Bp, Bs, Ls = xp.shape[0], xs.shape[0], xs.shape[1]
    rows = lambda a: a.reshape(depth, 1, -1)
    lane_low = (jnp.arange(BRANCH_W) % LANES) < HEAD_DIM
    gq = jnp.tile(q_norm_g, (1, N_HEADS)) * (ATTN_SCALE * LOG2E)
    weights = {"w_in": w_in, "w_branch": w_branch, "w_out": w_out, "w_gate_up": w_gate_up, "w_down": w_down}
    wl = {name: w[0:1].astype(BF16) for name, w in weights.items()}
    p = {
        "g_mix": rows(norm_mix_g), "b_gate": rows(b_gate), "g_gmlp": rows(gmlp_norm_g), "ws": gmlp_ws,
        "gbias": jnp.repeat(jnp.swapaxes(gmlp_bs, 1, 2), GMLP_GW, axis=2),
        "conv_dw": conv_dw, "conv_b": rows(conv_b), "g_conv": rows(conv_norm_g),
        "gq_lo": rows(jnp.where(lane_low, gq, 0.0)), "gq_hi": rows(jnp.where(lane_low, 0.0, gq)),
        "g_k": rows(jnp.tile(k_norm_g, (1, N_HEADS))),
        "relb": _rel_bias_table(rel_bias.reshape(depth * N_HEADS, -1)).reshape(depth, N_PAIRS, 2 * CHUNK, BAND_PAD),
        "g_ffn": rows(norm_ffn_g),
    }
    kp_l, vp_l, cp_l, ks_l, vs_l, cs_l, gs_l = [], [], [], [], [], [], []
    for l in range(depth):
        xp, kp, vp, cp = _mixer(xp, None, p, wl, l, Ts=MIXER_TILE, n_seq=1, emit_vn=False)
        xp, wl_next = _ffn(xp, p, wl, l, T=FFN_TILE, cast_next=(weights, l + 1) if l + 1 < depth else None)
        xs, ks, vs, cs, gs = _mixer(xs, (cache_attn_k[l], cache_attn_v[l], cache_conv[l]), p, wl, l, Ts=Ls,
                                    n_seq=SAMPLE_SEQS, emit_vn=True)
        xs, _ = _ffn(xs, p, wl, l, T=Bs * Ls)
        wl = wl_next
        kp_l.append(kp.reshape(Bp, -1, N_HEADS, HEAD_DIM))
        vp_l.append(vp.reshape(Bp, -1, N_HEADS, HEAD_DIM))
        cp_l.append(cp)
        ks_l.append(ks.reshape(Bs, Ls, N_HEADS, HEAD_DIM))
        vs_l.append(vs.reshape(Bs, Ls, N_HEADS, HEAD_DIM))
        cs_l.append(cs)
        gs_l.append(gs)
    return (xp, xs, jnp.stack(kp_l), jnp.stack(vp_l), jnp.stack(cp_l),
            jnp.stack(ks_l), jnp.stack(vs_l), jnp.stack(cs_l), jnp.stack(gs_l))
```
